```python
import math
import jax, jax.numpy as jnp
from jax import lax
import numpy as np

D_MODEL = 1024
BATCH = 8
SEQ = 4096
DEPTH = 2

EPS = 1e-6
GLA_HEADS = 4
GLA_DV = D_MODEL // (2 * GLA_HEADS)
GLA_DK = GLA_DV // 2
GLA_RANK = 16
GLA_GATE_NORM = 16.0
GLA_CHUNK = 64
DSA_HEADS = 4
DSA_DH = D_MODEL // (2 * DSA_HEADS)
IDX_HEADS = 8
IDX_DIM = 64
TOPK_MAX = 256
Q_BLOCK = 128
RET_HEADS = 4
RET_DK = D_MODEL // RET_HEADS
RET_DV = 2 * RET_DK
RET_CHUNK = 64
ROPE_BASE = 10000.0
D_FF = ((8 * D_MODEL + 3 * 256 - 1) // (3 * 256)) * 256

EVEN_SPLITS = (GLA_HEADS * GLA_DK,
               GLA_HEADS * GLA_DK,
               GLA_HEADS * GLA_DV,
               GLA_RANK,
               GLA_HEADS * GLA_DV,
               DSA_HEADS * DSA_DH,
               DSA_DH,
               DSA_DH,
               IDX_HEADS * IDX_DIM,
               IDX_DIM,
               IDX_HEADS)
EVEN_IN = sum(EVEN_SPLITS)
EVEN_MIX = GLA_HEADS * GLA_DV + DSA_HEADS * DSA_DH
ODD_SPLITS = (RET_HEADS * RET_DK, RET_HEADS * RET_DK, RET_HEADS * RET_DV, RET_HEADS * RET_DV)
ODD_IN = sum(ODD_SPLITS)
ODD_MIX = RET_HEADS * RET_DV

kernel_name = "hybrid_gla_dsa_retention_trunk"


def _split(z, sizes):
    out, start = [], 0
    for s in sizes:
        out.append(z[..., start:start + s])
        start += s
    return out


def rmsnorm(x, g):
    xf = x.astype(jnp.float32)
    y = xf * lax.rsqrt(jnp.mean(xf * xf, axis=-1, keepdims=True) + EPS)
    return (y * g.astype(jnp.float32)).astype(x.dtype)


def _to_chunks(z, c):
    b, t, h, d = z.shape
    return z.astype(jnp.float32).reshape(b, t // c, c, h, d).transpose(1, 0, 3, 2, 4)


def _from_chunks(z):
    n, b, h, c, d = z.shape
    return z.transpose(1, 0, 3, 2, 4).reshape(b, n * c, h, d)


def gla_chunked(q, k, v, log_a):
    b_, t_, h_, dk = q.shape
    dv = v.shape[-1]
    c = GLA_CHUNK
    qc, kc, vc, gc = (_to_chunks(z, c) for z in (q, k, v, log_a))
    cum = jnp.cumsum(gc, axis=3)
    last = cum[:, :, :, -1:, :]
    q_t = qc * jnp.exp(cum) * (dk ** -0.5)
    k_t = kc * jnp.exp(-cum)
    k_end = kc * jnp.exp(last - cum)
    causal = jnp.tril(jnp.ones((c, c), dtype=bool))
    a = jnp.einsum('nbhid,nbhjd->nbhij', q_t, k_t)
    a = jnp.where(causal, a, 0.0)
    o_intra = jnp.einsum('nbhij,nbhjv->nbhiv', a, vc)

    def step(state, inp):
        qq, ke, vv, ll = inp
        o = jnp.einsum('bhid,bhdv->bhiv', qq, state)
        state = jnp.exp(ll)[:, :, 0, :, None] * state + jnp.einsum('bhjd,bhjv->bhdv', ke, vv)
        return state, o

    s0 = jnp.zeros((b_, h_, dk, dv), jnp.float32)
    _, o_inter = lax.scan(step, s0, (q_t, k_end, vc, last))
    return _from_chunks(o_intra + o_inter).astype(q.dtype)


def dsa_attention(q, k, v, iq, ik, iw, topk):
    b_, t_, h_, dh = q.shape
    nb = t_ // Q_BLOCK
    key_pos = jnp.arange(t_)
    ikf = ik.astype(jnp.float32)

    def blk(z):
        return z.reshape(b_, nb, Q_BLOCK, *z.shape[2:]).swapaxes(0, 1)

    def one_block(args):
        qb, iqb, iwb, bi = args
        q_pos = bi * Q_BLOCK + jnp.arange(Q_BLOCK)
        allowed = key_pos[None, :] <= q_pos[:, None]
        logits = jnp.einsum('bqhd,bsd->bqhs', iqb.astype(jnp.float32), ikf) * (IDX_DIM ** -0.5)
        score = jnp.einsum('bqh,bqhs->bqs', iwb.astype(jnp.float32), jax.nn.relu(logits))
        score = jnp.where(allowed[None], score, -jnp.inf)
        _, idx = lax.top_k(score, topk)
        valid = idx <= q_pos[None, :, None]
        k_sel = jax.vmap(lambda kk, ii: kk[ii])(k, idx).astype(jnp.float32)
        v_sel = jax.vmap(lambda vv, ii: vv[ii])(v, idx).astype(jnp.float32)
        s = jnp.einsum('bqhd,bqkd->bqhk', qb.astype(jnp.float32), k_sel) * (dh ** -0.5)
        s = jnp.where(valid[:, :, None, :], s, -jnp.inf)
        p = jax.nn.softmax(s, axis=-1)
        return jnp.einsum('bqhk,bqkd->bqhd', p, v_sel).astype(q.dtype)

    out = lax.map(one_block, (blk(q), blk(iq), blk(iw), jnp.arange(nb)))
    return out.swapaxes(0, 1).reshape(b_, t_, h_, dh)


def rotary(x, pos):
    half = x.shape[-1] // 2
    inv = ROPE_BASE ** (-jnp.arange(half, dtype=jnp.float32) / half)
    ang = pos.astype(jnp.float32)[:, None] * inv[None, :]
    cos = jnp.cos(ang)[None, :, None, :]
    sin = jnp.sin(ang)[None, :, None, :]
    xf = x.astype(jnp.float32)
    x1, x2 = xf[..., :half], xf[..., half:]
    return jnp.concatenate([x1 * cos - x2 * sin, x2 * cos + x1 * sin], axis=-1)


def retention_chunked(q, k, v):
    b_, t_, h_, dk = q.shape
    dv = v.shape[-1]
    c = RET_CHUNK
    log_g = jnp.log1p(-jnp.exp2(-5.0 - jnp.arange(h_, dtype=jnp.float32)))
    qc = _to_chunks(q, c)
    kc = _to_chunks(k, c) * (dk ** -0.5)
    vc = _to_chunks(v, c)
    idx = jnp.arange(c, dtype=jnp.float32)
    rel = idx[:, None] - idx[None, :]
    dmat = jnp.where(rel >= 0, jnp.exp(log_g[:, None, None] * jnp.maximum(rel, 0.0)), 0.0)
    inner = jnp.einsum('nbhid,nbhjd->nbhij', qc, kc) * dmat[None, None]
    o_intra = jnp.einsum('nbhij,nbhjv->nbhiv', inner, vc)
    xi = jnp.exp(log_g[:, None] * (idx[None, :] + 1.0))
    zeta = jnp.exp(log_g[:, None] * (c - 1.0 - idx[None, :]))
    decay_c = jnp.exp(log_g * c)

    def step(state, inp):
        qq, kk, vv = inp
        o = jnp.einsum('bhid,bhdv->bhiv', qq, state) * xi[None, :, :, None]
        state = decay_c[None, :, None, None] * state + jnp.einsum(
            'bhjd,bhjv->bhdv', kk * zeta[None, :, :, None], vv)
        return state, o

    s0 = jnp.zeros((b_, h_, dk, dv), jnp.float32)
    _, o_inter = lax.scan(step, s0, (qc, kc, vc))
    return _from_chunks(o_intra + o_inter).astype(q.dtype)


def even_mixer(h, w_in, gla_wa2, gla_ba2, gla_norm, w_out, topk):
    b_, t_, _ = h.shape
    proj = h @ w_in
    gq, gk, gv, ga, gr, dq, dk_, dv_, iq, ik, iw = _split(proj, EVEN_SPLITS)
    log_a = jax.nn.log_sigmoid((ga @ gla_wa2 + gla_ba2).astype(jnp.float32)) / GLA_GATE_NORM
    o_gla = gla_chunked(gq.reshape(b_, t_, GLA_HEADS, GLA_DK),
                        gk.reshape(b_, t_, GLA_HEADS, GLA_DK),
                        gv.reshape(b_, t_, GLA_HEADS, GLA_DV),
                        log_a.reshape(b_, t_, GLA_HEADS, GLA_DK))
    o_gla = rmsnorm(o_gla, gla_norm).reshape(b_, t_, GLA_HEADS * GLA_DV) * jax.nn.silu(gr)
    o_dsa = dsa_attention(dq.reshape(b_, t_, DSA_HEADS, DSA_DH), dk_, dv_,
                          iq.reshape(b_, t_, IDX_HEADS, IDX_DIM), ik,
                          iw * (IDX_HEADS ** -0.5), topk)
    o_dsa = o_dsa.reshape(b_, t_, DSA_HEADS * DSA_DH)
    return jnp.concatenate([o_gla.astype(h.dtype), o_dsa.astype(h.dtype)], axis=-1) @ w_out


def odd_mixer(h, w_in, ret_norm, w_out, pos):
    b_, t_, _ = h.shape
    q, k, v, g = _split(h @ w_in, ODD_SPLITS)
    q = rotary(q.reshape(b_, t_, RET_HEADS, RET_DK), pos)
    k = rotary(k.reshape(b_, t_, RET_HEADS, RET_DK), pos)
    o = retention_chunked(q, k, v.reshape(b_, t_, RET_HEADS, RET_DV))
    o = rmsnorm(o, ret_norm).reshape(b_, t_, ODD_MIX) * jax.nn.silu(g)
    return o.astype(h.dtype) @ w_out


def swiglu(h, w_gate, w_up, w_down):
    return (jax.nn.silu(h @ w_gate) * (h @ w_up)) @ w_down


def setup_inputs(seed: int = 0) -> dict:
    key = jax.random.key(seed)
    ks = jax.random.split(key, 20)
    ne = (DEPTH + 1) // 2
    no = DEPTH // 2
    f32 = jnp.float32

    def dense(k, shape, fan_in):
        return jax.random.normal(k, shape, f32) * (fan_in ** -0.5)

    def gain(k, shape):
        return 1.0 + 0.02 * jax.random.normal(k, shape, f32)

    return {
        "x": jax.random.normal(ks[0], (BATCH, SEQ, D_MODEL), f32),
        "even_attn_norm": gain(ks[1], (ne, D_MODEL)),
        "even_w_in": dense(ks[2], (ne, D_MODEL, EVEN_IN), D_MODEL),
        "even_gla_wa2": dense(ks[3], (ne, GLA_RANK, GLA_HEADS * GLA_DK), GLA_RANK),
        "even_gla_ba2": 0.1 * jax.random.normal(ks[4], (ne, GLA_HEADS * GLA_DK), f32),
        "even_gla_norm": gain(ks[5], (ne, GLA_DV)),
        "even_w_out": dense(ks[6], (ne, EVEN_MIX, D_MODEL), EVEN_MIX),
        "odd_attn_norm": gain(ks[7], (no, D_MODEL)),
        "odd_w_in": dense(ks[8], (no, D_MODEL, ODD_IN), D_MODEL),
        "odd_ret_norm": gain(ks[9], (no, RET_DV)),
        "odd_w_out": dense(ks[10], (no, ODD_MIX, D_MODEL), ODD_MIX),
        "ffn_norm": gain(ks[11], (DEPTH, D_MODEL)),
        "ffn_w_gate": dense(ks[12], (DEPTH, D_MODEL, D_FF), D_MODEL),
        "ffn_w_up": dense(ks[13], (DEPTH, D_MODEL, D_FF), D_MODEL),
        "ffn_w_down": dense(ks[14], (DEPTH, D_FF, D_MODEL), D_FF),
        "final_norm": gain(ks[15], (D_MODEL,)),
    }


def reference(x, even_attn_norm, even_w_in, even_gla_wa2, even_gla_ba2, even_gla_norm, even_w_out,
              odd_attn_norm, odd_w_in, odd_ret_norm, odd_w_out,
              ffn_norm, ffn_w_gate, ffn_w_up, ffn_w_down, final_norm):
    t_ = x.shape[1]
    topk = min(TOPK_MAX, t_ // 4)
    pos = jnp.arange(t_, dtype=jnp.int32)
    h = x
    for layer in range(DEPTH):
        i = layer // 2
        if layer % 2 == 0:
            h = h + even_mixer(rmsnorm(h, even_attn_norm[i]), even_w_in[i], even_gla_wa2[i],
                               even_gla_ba2[i], even_gla_norm[i], even_w_out[i], topk)
        else:
            h = h + odd_mixer(rmsnorm(h, odd_attn_norm[i]), odd_w_in[i], odd_ret_norm[i],
                              odd_w_out[i], pos)
        h = h + swiglu(rmsnorm(h, ffn_norm[layer]), ffn_w_gate[layer], ffn_w_up[layer],
                       ffn_w_down[layer])
    return rmsnorm(h, final_norm)
```

```python
import functools

import jax
import jax.numpy as jnp
from jax import lax
from jax.experimental import pallas as pl
from jax.experimental.pallas import tpu as pltpu

F32 = jnp.float32
BF16 = jnp.bfloat16

D_MODEL = 1024
EPS = 1e-6
GLA_HEADS = 4
GLA_DV = 128
GLA_DK = 64
GLA_RANK = 16
GLA_GATE_NORM = 16.0
GLA_CHUNK = 64
DSA_HEADS = 4
DSA_DH = 128
IDX_HEADS = 8
IDX_DIM = 64
TOPK_MAX = 256
RET_HEADS = 4
RET_DK = 256
RET_DV = 512
ROPE_BASE = 10000.0
D_FF = 2816

LANES = 128
VMEM_LIMIT = 56 * 1024 * 1024

PROJ_TM = 512
FFN_TM = 512
FFN_TF = 1408
GLA_TT = 512
RET_CHUNK = 256
RET_TT = 512
DSA_TQ = 128
DSA_TK = 512
BISECT_ITERS = 34


def _cparams(*sem):
    return pltpu.CompilerParams(dimension_semantics=sem, vmem_limit_bytes=VMEM_LIMIT)


def _dot(a, b):
    return jnp.dot(a, b, preferred_element_type=F32)


def _dot_nt(a, b):
    return lax.dot_general(a, b, (((1,), (1,)), ((), ())), preferred_element_type=F32)


def _dot_tn(a, b):
    return lax.dot_general(a, b, (((0,), (0,)), ((), ())), preferred_element_type=F32)


def _rms(x, g):
    return x * lax.rsqrt(jnp.mean(x * x, axis=-1, keepdims=True) + EPS) * g


def _silu(x):
    return x * jax.nn.sigmoid(x)


def _full(shape):
    return pl.BlockSpec(shape, lambda *_: (0,) * len(shape))


_E_GQ, _E_GK, _E_GV, _E_GA, _E_GR, _E_DQ, _E_DK, _E_DV, _E_IQ, _E_IKW, _E_END = (
    0, 256, 512, 1024, 1152, 1664, 2176, 2304, 2432, 2944, 3072)


def _even_proj_kernel(x_ref, g_ref, w_ref, wa2_ref, ba2_ref,
                      gq_ref, gk_ref, gv_ref, la_ref, gr_ref,
                      dq_ref, dk_ref, dv_ref, iq_ref, ikw_ref):
    xn = _rms(x_ref[...], g_ref[...]).astype(BF16)

    def proj(a, b):
        return _dot(xn, w_ref[:, a:b])

    gq_ref[...] = proj(_E_GQ, _E_GK)
    gk_ref[...] = proj(_E_GK, _E_GV)
    gv_ref[...] = proj(_E_GV, _E_GA).astype(BF16)
    ga = proj(_E_GA, _E_GR).astype(BF16)
    z = _dot(ga, wa2_ref[...]) + ba2_ref[...]
    la_ref[...] = jax.nn.log_sigmoid(z) * (1.0 / GLA_GATE_NORM)
    gr_ref[...] = proj(_E_GR, _E_DQ)
    dq_ref[...] = (proj(_E_DQ, _E_DK) * (DSA_DH ** -0.5)).astype(BF16)
    dk_ref[...] = proj(_E_DK, _E_DV).astype(BF16)
    dv_ref[...] = proj(_E_DV, _E_IQ).astype(BF16)
    iq_ref[...] = proj(_E_IQ, _E_IKW).astype(BF16)
    ikw_ref[...] = proj(_E_IKW, _E_END)


def _even_proj(x2, gain, w_in, wa2, ba2):
    m = x2.shape[0]
    tm = PROJ_TM
    s = [0, 256, 512, 1024, 1040, 1552, 2064, 2192, 2320, 2832, 2896, 2904]
    cols = [w_in[:, s[i]:s[i + 1]] for i in range(11)]
    zpad = lambda n: jnp.zeros((D_MODEL, n), w_in.dtype)
    w = jnp.concatenate(
        cols[0:3] + [cols[3], zpad(LANES - GLA_RANK)] + cols[4:9]
        + [cols[9], cols[10], zpad(LANES - IDX_DIM - IDX_HEADS)], axis=1).astype(BF16)
    wa2p = jnp.concatenate(
        [wa2, jnp.zeros((LANES - GLA_RANK, wa2.shape[1]), wa2.dtype)], axis=0).astype(BF16)
    row = lambda n: pl.BlockSpec((tm, n), lambda i: (i, 0))
    outs = [(256, F32), (256, F32), (512, BF16), (256, F32), (512, F32),
            (512, BF16), (128, BF16), (128, BF16), (512, BF16), (128, F32)]
    return pl.pallas_call(
        _even_proj_kernel,
        grid=(m // tm,),
        in_specs=[row(D_MODEL), _full((1, D_MODEL)), _full((D_MODEL, _E_END)),
                  _full((LANES, 256)), _full((1, 256))],
        out_specs=[row(n) for n, _ in outs],
        out_shape=[jax.ShapeDtypeStruct((m, n), dt) for n, dt in outs],
        compiler_params=_cparams("parallel"),
        name="even_proj",
    )(x2, gain.reshape(1, D_MODEL), w, wa2p, ba2.reshape(1, 256))


def _gla_kernel(q_ref, k_ref, v_ref, la_ref, gr_ref, gn_ref, o_ref, st_ref, *, n_chunks):
    c_ = GLA_CHUNK

    @pl.when(pl.program_id(1) == 0)
    def _():
        st_ref[...] = jnp.zeros_like(st_ref)

    ri = lax.broadcasted_iota(jnp.int32, (c_, c_), 0)
    ci = lax.broadcasted_iota(jnp.int32, (c_, c_), 1)
    tri = ri >= ci
    ltri = jnp.where(tri, 1.0, 0.0).astype(BF16)
    gn = gn_ref[...]

    def chunk(c, carry):
        rows = pl.ds(pl.multiple_of(c * c_, c_), c_)
        g = la_ref[rows, :]
        g1 = g.astype(BF16)
        r1 = g - g1.astype(F32)
        g2 = r1.astype(BF16)
        g3 = (r1 - g2.astype(F32)).astype(BF16)
        cum = _dot(ltri, g1) + _dot(ltri, g2) + _dot(ltri, g3)
        last = cum[c_ - 1:c_, :]
        kk = k_ref[rows, :]
        q_t = (q_ref[rows, :] * jnp.exp(cum) * (GLA_DK ** -0.5)).astype(BF16)
        k_t = (kk * jnp.exp(-cum)).astype(BF16)
        k_e = (kk * jnp.exp(last - cum)).astype(BF16)
        dec = jnp.exp(last)
        for h in range(GLA_HEADS):
            ks = slice(h * GLA_DK, (h + 1) * GLA_DK)
            vs = slice(h * GLA_DV, (h + 1) * GLA_DV)
            v_h = v_ref[rows, vs]
            a = jnp.where(tri, _dot_nt(q_t[:, ks], k_t[:, ks]), 0.0).astype(BF16)
            st = st_ref[h]
            o = _dot(a, v_h) + _dot_nt(q_t[:, ks], st.astype(BF16))
            st_ref[h] = dec[:, ks] * st + _dot_tn(v_h, k_e[:, ks])
            y = _rms(o, gn) * _silu(gr_ref[rows, vs])
            o_ref[rows, vs] = y.astype(o_ref.dtype)
        return carry

    lax.fori_loop(0, n_chunks, chunk, 0)


def _gla(gq, gk, gv, la, gr, gnorm, b, t):
    tt = min(GLA_TT, t)
    nt = t // tt
    row = lambda n: pl.BlockSpec((tt, n), lambda i, j: (i * nt + j, 0))
    return pl.pallas_call(
        functools.partial(_gla_kernel, n_chunks=tt // GLA_CHUNK),
        grid=(b, nt),
        in_specs=[row(256), row(256), row(512), row(256), row(512), _full((1, GLA_DV))],
        out_specs=row(512),
        out_shape=jax.ShapeDtypeStruct((b * t, 512), BF16),
        scratch_shapes=[pltpu.VMEM((GLA_HEADS, GLA_DV, GLA_DK), F32)],
        compiler_params=_cparams("parallel", "arbitrary"),
        name="gla",
    )(gq, gk, gv, la, gr, gnorm.reshape(1, GLA_DV))


def _dsa_kernel(dq_ref, iq_ref, iwq_ref, dk_ref, dv_ref, ikw_ref, o_ref,
                s_ref, iqs_ref, wb_ref, m_ref, l_ref, acc_ref, jl_ref, *, topk, t_len):
    tq, tk = DSA_TQ, DSA_TK
    qi = pl.program_id(1)
    q0 = qi * tq
    nkt = (q0 + tq + tk - 1) // tk
    neg_inf = -jnp.inf
    rep = lambda x: jnp.concatenate([x] * (tk // LANES), axis=1)

    wscale = (IDX_HEADS ** -0.5) * (IDX_DIM ** -0.5)
    for h in range(IDX_HEADS):
        iqs_ref[h] = iq_ref[:, h * IDX_DIM:(h + 1) * IDX_DIM]
        wcol = iwq_ref[:, IDX_DIM + h:IDX_DIM + h + 1] * wscale
        wb_ref[h] = jnp.broadcast_to(wcol, (tq, LANES))

    qpos = q0 + lax.broadcasted_iota(jnp.int32, (tq, tk), 0)
    kiota = lax.broadcasted_iota(jnp.int32, (tq, tk), 1)

    def score_tile(kt, carry):
        rmax, rmin = carry
        k0 = pl.multiple_of(kt * tk, tk)
        ik_t = ikw_ref[pl.ds(k0, tk), :][:, :IDX_DIM].astype(BF16)
        sc = jnp.zeros((tq, tk), F32)
        for h in range(IDX_HEADS):
            lg = _dot_nt(iqs_ref[h], ik_t)
            sc = sc + rep(wb_ref[h]) * jnp.maximum(lg, 0.0)
        allowed = (k0 + kiota) <= qpos
        s_ref[kt] = jnp.where(allowed, sc, neg_inf)
        rmax = jnp.maximum(rmax, jnp.max(jnp.where(allowed, sc, neg_inf), axis=1, keepdims=True))
        rmin = jnp.minimum(rmin, jnp.min(jnp.where(allowed, sc, jnp.inf), axis=1, keepdims=True))
        return rmax, rmin

    rmax, rmin = lax.fori_loop(
        0, nkt, score_tile,
        (jnp.full((tq, 1), neg_inf, F32), jnp.full((tq, 1), jnp.inf, F32)))

    def count(ind_fn):
        def body(kt, acc):
            c = ind_fn(s_ref[kt], kt)
            for j in range(tk // LANES):
                acc = acc + c[:, j * LANES:(j + 1) * LANES]
            return acc
        acc = lax.fori_loop(0, nkt, body, jnp.zeros((tq, LANES), F32))
        return jnp.sum(acc, axis=1, keepdims=True)

    kf = jnp.float32(topk)

    def bisect(_, carry):
        lo, hi = carry
        mid = lo + (hi - lo) * 0.5
        midb = rep(jnp.broadcast_to(mid, (tq, LANES)))
        ge = count(lambda s, kt: jnp.where(s >= midb, 1.0, 0.0)) >= kf
        return jnp.where(ge, mid, lo), jnp.where(ge, hi, mid)

    lo, hi = lax.fori_loop(0, BISECT_ITERS, bisect, (rmin, rmax))
    lob = rep(jnp.broadcast_to(lo, (tq, LANES)))

    def snap(kt, acc):
        s = s_ref[kt]
        c = jnp.where(s >= lob, s, jnp.inf)
        for j in range(tk // LANES):
            acc = jnp.minimum(acc, c[:, j * LANES:(j + 1) * LANES])
        return acc

    thr = jnp.min(lax.fori_loop(0, nkt, snap, jnp.full((tq, LANES), jnp.inf, F32)),
                  axis=1, keepdims=True)
    thrb = rep(jnp.broadcast_to(thr, (tq, LANES)))
    n_gt = count(lambda s, kt: jnp.where(s > thrb, 1.0, 0.0))
    n_eq = count(lambda s, kt: jnp.where(s == thrb, 1.0, 0.0))
    need = kf - n_gt

    jl_ref[...] = jnp.full((tq, LANES), t_len, jnp.int32)

    @pl.when(jnp.max(n_eq - need) > 0.0)
    def _():
        def ibisect(_, carry):
            ilo, ihi = carry
            imid = jnp.right_shift(ilo + ihi, 1)
            imb = rep(jnp.broadcast_to(imid, (tq, LANES)))
            c = count(lambda s, kt: jnp.where(
                s == thrb, jnp.where((kt * tk + kiota) <= imb, 1.0, 0.0), 0.0))
            ge = c >= need
            return jnp.where(ge, ilo, imid), jnp.where(ge, imid, ihi)
        ilo0 = jnp.full((tq, 1), -1, jnp.int32)
        ihi0 = jnp.full((tq, 1), t_len - 1, jnp.int32)
        n_it = max(1, int(t_len - 1).bit_length() + 1)
        _, ihi = lax.fori_loop(0, n_it, ibisect, (ilo0, ihi0))
        jl_ref[...] = jnp.broadcast_to(ihi, (tq, LANES))

    jlb = rep(jl_ref[...])

    m_ref[...] = jnp.full(m_ref.shape, neg_inf, F32)
    l_ref[...] = jnp.zeros(l_ref.shape, F32)
    acc_ref[...] = jnp.zeros(acc_ref.shape, F32)

    def attend(kt, carry):
        k0 = pl.multiple_of(kt * tk, tk)
        s = s_ref[kt]
        keep_tie = jnp.where((k0 + kiota) <= jlb, 0.0, neg_inf)
        bias = jnp.where(s > thrb, 0.0, jnp.where(s == thrb, keep_tie, neg_inf))
        k_t = dk_ref[pl.ds(k0, tk), :]
        v_t = dv_ref[pl.ds(k0, tk), :]
        for h in range(DSA_HEADS):
            hs = slice(h * DSA_DH, (h + 1) * DSA_DH)
            sh = _dot_nt(dq_ref[:, hs], k_t) + bias
            m_old = m_ref[h]
            m_new = jnp.maximum(m_old, jnp.max(sh, axis=1, keepdims=True))
            m_safe = jnp.where(m_new == neg_inf, 0.0, m_new)
            alpha = jnp.exp(m_old - m_safe)
            p = jnp.exp(sh - m_safe)
            l_ref[h] = alpha * l_ref[h] + jnp.sum(p, axis=1, keepdims=True)
            acc_ref[h] = alpha * acc_ref[h] + _dot(p.astype(BF16), v_t)
            m_ref[h] = m_new
        return carry

    lax.fori_loop(0, nkt, attend, 0)
    for h in range(DSA_HEADS):
        o_ref[:, h * DSA_DH:(h + 1) * DSA_DH] = (acc_ref[h] / l_ref[h]).astype(o_ref.dtype)


def _dsa(dq, dk, dv, iq, ikw, b, t):
    tq, tk = DSA_TQ, DSA_TK
    nq = t // tq
    topk = min(TOPK_MAX, t // 4)
    qrow = lambda n: pl.BlockSpec((tq, n), lambda i, j: (i * nq + j, 0))
    krow = lambda n: pl.BlockSpec((t, n), lambda i, j: (i, 0))
    return pl.pallas_call(
        functools.partial(_dsa_kernel, topk=topk, t_len=t),
        grid=(b, nq),
        in_specs=[qrow(512), qrow(512), qrow(LANES), krow(DSA_DH), krow(DSA_DH), krow(LANES)],
        out_specs=qrow(512),
        out_shape=jax.ShapeDtypeStruct((b * t, 512), BF16),
        scratch_shapes=[
            pltpu.VMEM((pl.cdiv(t, tk), tq, tk), F32),
            pltpu.VMEM((IDX_HEADS, tq, IDX_DIM), BF16),
            pltpu.VMEM((IDX_HEADS, tq, LANES), F32),
            pltpu.VMEM((DSA_HEADS, tq, 1), F32),
            pltpu.VMEM((DSA_HEADS, tq, 1), F32),
            pltpu.VMEM((DSA_HEADS, tq, DSA_DH), F32),
            pltpu.VMEM((tq, LANES), jnp.int32),
        ],
        compiler_params=_cparams("parallel", "arbitrary"),
        name="dsa",
    )(dq, iq, ikw, dk, dv, ikw)


def _out_proj_kernel(*refs, n_in):
    res_ref = refs[0]
    a_refs = refs[1:1 + n_in]
    w_refs = refs[1 + n_in:1 + 2 * n_in]
    o_ref = refs[1 + 2 * n_in]
    acc = res_ref[...]
    for a, w in zip(a_refs, w_refs):
        acc = acc + _dot(a[...], w[...])
    o_ref[...] = acc


def _out_proj(res, acts, ws):
    m = res.shape[0]
    tm = PROJ_TM
    row = lambda n: pl.BlockSpec((tm, n), lambda i: (i, 0))
    return pl.pallas_call(
        functools.partial(_out_proj_kernel, n_in=len(acts)),
        grid=(m // tm,),
        in_specs=[row(D_MODEL)] + [row(a.shape[1]) for a in acts] + [_full(w.shape) for w in ws],
        out_specs=row(D_MODEL),
        out_shape=jax.ShapeDtypeStruct((m, D_MODEL), F32),
        compiler_params=_cparams("parallel"),
        name="out_proj",
    )(res, *acts, *ws)


def _ffn_kernel(h_ref, g_ref, wg_ref, wu_ref, wd_ref, fg_ref, o_ref, xn_ref, acc_ref, *, final_norm):
    f = pl.program_id(1)

    @pl.when(f == 0)
    def _():
        xn_ref[...] = _rms(h_ref[...], g_ref[...]).astype(BF16)
        acc_ref[...] = h_ref[...]

    xn = xn_ref[...]
    a = _silu(_dot(xn, wg_ref[...])) * _dot(xn, wu_ref[...])
    acc_ref[...] += _dot(a.astype(BF16), wd_ref[...])

    @pl.when(f == pl.num_programs(1) - 1)
    def _():
        y = acc_ref[...]
        o_ref[...] = _rms(y, fg_ref[...]) if final_norm else y


def _ffn(h, gain, wg, wu, wd, final_gain=None):
    m = h.shape[0]
    tm, tf = FFN_TM, FFN_TF
    fg = jnp.ones((1, D_MODEL), F32) if final_gain is None else final_gain.reshape(1, D_MODEL)
    return pl.pallas_call(
        functools.partial(_ffn_kernel, final_norm=final_gain is not None),
        grid=(m // tm, D_FF // tf),
        in_specs=[pl.BlockSpec((tm, D_MODEL), lambda i, j: (i, 0)),
                  _full((1, D_MODEL)),
                  pl.BlockSpec((D_MODEL, tf), lambda i, j: (0, j)),
                  pl.BlockSpec((D_MODEL, tf), lambda i, j: (0, j)),
                  pl.BlockSpec((tf, D_MODEL), lambda i, j: (j, 0)),
                  _full((1, D_MODEL))],
        out_specs=pl.BlockSpec((tm, D_MODEL), lambda i, j: (i, 0)),
        out_shape=jax.ShapeDtypeStruct((m, D_MODEL), F32),
        scratch_shapes=[pltpu.VMEM((tm, D_MODEL), BF16), pltpu.VMEM((tm, D_MODEL), F32)],
        compiler_params=_cparams("parallel", "arbitrary"),
        name="ffn",
    )(h, gain.reshape(1, D_MODEL), wg.astype(BF16), wu.astype(BF16), wd.astype(BF16), fg)


def _odd_proj_kernel(x_ref, g_ref, w_ref, cos_ref, sin_ref, q_ref, k_ref, v_ref, gate_ref):
    xn = _rms(x_ref[...], g_ref[...]).astype(BF16)
    cos = cos_ref[...]
    sin = sin_ref[...]
    half = RET_DK // 2

    def rot(z, out_ref, scale):
        for h in range(RET_HEADS):
            x1 = z[:, h * RET_DK:h * RET_DK + half]
            x2 = z[:, h * RET_DK + half:(h + 1) * RET_DK]
            out_ref[:, h * RET_DK:h * RET_DK + half] = ((x1 * cos - x2 * sin) * scale).astype(BF16)
            out_ref[:, h * RET_DK + half:(h + 1) * RET_DK] = ((x2 * cos + x1 * sin) * scale).astype(BF16)

    nq = RET_HEADS * RET_DK
    nv = RET_HEADS * RET_DV
    rot(_dot(xn, w_ref[:, 0:nq]), q_ref, 1.0)
    rot(_dot(xn, w_ref[:, nq:2 * nq]), k_ref, RET_DK ** -0.5)
    v_ref[...] = _dot(xn, w_ref[:, 2 * nq:2 * nq + nv]).astype(BF16)
    gate_ref[...] = _dot(xn, w_ref[:, 2 * nq + nv:2 * nq + 2 * nv])


def _odd_proj(h, gain, w_in, t):
    m = h.shape[0]
    tm = PROJ_TM
    nt = t // tm
    half = RET_DK // 2
    inv = ROPE_BASE ** (-jnp.arange(half, dtype=F32) / half)
    ang = jnp.arange(t, dtype=jnp.int32).astype(F32)[:, None] * inv[None, :]
    row = lambda n: pl.BlockSpec((tm, n), lambda i: (i, 0))
    pos = pl.BlockSpec((tm, half), lambda i: (i % nt, 0))
    nq = RET_HEADS * RET_DK
    nv = RET_HEADS * RET_DV
    return pl.pallas_call(
        _odd_proj_kernel,
        grid=(m // tm,),
        in_specs=[row(D_MODEL), _full((1, D_MODEL)), _full(w_in.shape), pos, pos],
        out_specs=[row(nq), row(nq), row(nv), row(nv)],
        out_shape=[jax.ShapeDtypeStruct((m, nq), BF16), jax.ShapeDtypeStruct((m, nq), BF16),
                   jax.ShapeDtypeStruct((m, nv), BF16), jax.ShapeDtypeStruct((m, nv), F32)],
        compiler_params=_cparams("parallel"),
        name="odd_proj",
    )(h, gain.reshape(1, D_MODEL), w_in.astype(BF16), jnp.cos(ang), jnp.sin(ang))


def _ret_kernel(dec_ref, q_ref, k_ref, v_ref, gate_ref, dm_ref, xi_ref, ze_ref, rn_ref,
                o_ref, st_ref, *, n_chunks):
    c_ = RET_CHUNK

    @pl.when(pl.program_id(1) == 0)
    def _():
        st_ref[...] = jnp.zeros_like(st_ref)

    rn = rn_ref[...]
    for c in range(n_chunks):
        rows = slice(c * c_, (c + 1) * c_)
        for h in range(RET_HEADS):
            ks = slice(h * RET_DK, (h + 1) * RET_DK)
            vs = slice(h * RET_DV, (h + 1) * RET_DV)
            q = q_ref[rows, ks]
            k = k_ref[rows, ks]
            v = v_ref[rows, vs]
            st = st_ref[h]
            inner = (_dot_nt(q, k) * dm_ref[h]).astype(BF16)
            o = _dot(inner, v) + xi_ref[h] * _dot(q, st.astype(BF16))
            kz = (k.astype(F32) * ze_ref[h]).astype(BF16)
            st_ref[h] = dec_ref[h] * st + _dot_tn(kz, v)
            y = _rms(o, rn) * _silu(gate_ref[rows, vs])
            o_ref[rows, vs] = y.astype(o_ref.dtype)


def _retention(q, k, v, gate, rnorm, b, t):
    c_ = min(RET_CHUNK, t)
    tt = min(RET_TT, t)
    nt = t // tt
    log_g = jnp.log1p(-jnp.exp2(-5.0 - jnp.arange(RET_HEADS, dtype=F32)))
    idx = jnp.arange(c_, dtype=F32)
    rel = idx[:, None] - idx[None, :]
    dmat = jnp.where(rel >= 0, jnp.exp(log_g[:, None, None] * jnp.maximum(rel, 0.0)), 0.0)
    xi = jnp.exp(log_g[:, None] * (idx[None, :] + 1.0))[:, :, None]
    zeta = jnp.exp(log_g[:, None] * (c_ - 1.0 - idx[None, :]))[:, :, None]
    decay_c = jnp.exp(log_g * c_)
    nq = RET_HEADS * RET_DK
    nv = RET_HEADS * RET_DV
    row = lambda n: pl.BlockSpec((tt, n), lambda i, j: (i * nt + j, 0))
    return pl.pallas_call(
        functools.partial(_ret_kernel, n_chunks=tt // c_),
        grid=(b, nt),
        in_specs=[pl.BlockSpec(memory_space=pltpu.SMEM),
                  row(nq), row(nq), row(nv), row(nv),
                  _full((RET_HEADS, c_, c_)), _full((RET_HEADS, c_, 1)), _full((RET_HEADS, c_, 1)),
                  _full((1, RET_DV))],
        out_specs=row(nv),
        out_shape=jax.ShapeDtypeStruct((b * t, nv), BF16),
        scratch_shapes=[pltpu.VMEM((RET_HEADS, RET_DK, RET_DV), F32)],
        compiler_params=_cparams("parallel", "arbitrary"),
        name="retention",
    )(decay_c, q, k, v, gate, dmat, xi, zeta, rnorm.reshape(1, RET_DV))


def kernel(x, even_attn_norm, even_w_in, even_gla_wa2, even_gla_ba2, even_gla_norm, even_w_out,
           odd_attn_norm, odd_w_in, odd_ret_norm, odd_w_out,
           ffn_norm, ffn_w_gate, ffn_w_up, ffn_w_down, final_norm):
    b, t, d = x.shape
    h = x.reshape(b * t, d)

    gq, gk, gv, la, gr, dq, dk, dv, iq, ikw = _even_proj(
        h, even_attn_norm[0], even_w_in[0], even_gla_wa2[0], even_gla_ba2[0])
    o_gla = _gla(gq, gk, gv, la, gr, even_gla_norm[0], b, t)
    o_dsa = _dsa(dq, dk, dv, iq, ikw, b, t)
    w_out = even_w_out[0].astype(BF16)
    n_gla = GLA_HEADS * GLA_DV
    h = _out_proj(h, [o_gla, o_dsa], [w_out[:n_gla], w_out[n_gla:]])
    h = _ffn(h, ffn_norm[0], ffn_w_gate[0], ffn_w_up[0], ffn_w_down[0])

    q, k, v, gate = _odd_proj(h, odd_attn_norm[0], odd_w_in[0], t)
    o_ret = _retention(q, k, v, gate, odd_ret_norm[0], b, t)
    h = _out_proj(h, [o_ret], [odd_w_out[0].astype(BF16)])
    h = _ffn(h, ffn_norm[1], ffn_w_gate[1], ffn_w_up[1], ffn_w_down[1], final_gain=final_norm)
    return h.reshape(b, t, d)
```

```python
import functools

import jax
import jax.numpy as jnp
from jax import lax
from jax.experimental import pallas as pl
from jax.experimental.pallas import tpu as pltpu

F32 = jnp.float32
BF16 = jnp.bfloat16

D_MODEL = 1024
EPS = 1e-6
GLA_HEADS = 4
GLA_DV = 128
GLA_DK = 64
GLA_RANK = 16
GLA_GATE_NORM = 16.0
GLA_CHUNK = 64
DSA_HEADS = 4
DSA_DH = 128
IDX_HEADS = 8
IDX_DIM = 64
TOPK_MAX = 256
RET_HEADS = 4
RET_DK = 256
RET_DV = 512
ROPE_BASE = 10000.0
D_FF = 2816

LANES = 128
SUBLANES = 8
VMEM_LIMIT = 56 * 1024 * 1024

PROJ_TM = 512
FFN_TM = 512
FFN_TF = 1408
GLA_TT = 512
RET_CHUNK = 256
RET_TT = 512
DSA_TQ = 256
DSA_TK = 256
DSA_SK = 128
BISECT_ITERS = 34


def _cparams(*sem):
    return pltpu.CompilerParams(dimension_semantics=sem, vmem_limit_bytes=VMEM_LIMIT)


def _dot(a, b):
    return jnp.dot(a, b, preferred_element_type=F32)


def _dot_nt(a, b):
    return lax.dot_general(a, b, (((1,), (1,)), ((), ())), preferred_element_type=F32)


def _dot_tn(a, b):
    return lax.dot_general(a, b, (((0,), (0,)), ((), ())), preferred_element_type=F32)


def _rms(x, g):
    return x * lax.rsqrt(jnp.mean(x * x, axis=-1, keepdims=True) + EPS) * g


def _silu(x):
    return x * jax.nn.sigmoid(x)


def _full(shape):
    return pl.BlockSpec(shape, lambda *_: (0,) * len(shape))


_E_GQ, _E_GK, _E_GV, _E_GA, _E_GR, _E_DK, _E_IK, _E_END = (0, 256, 512, 1024, 1152, 1664, 1792, 1920)
_T_DQ, _T_IQ, _T_DV, _T_IW, _T_END = (0, 512, 1024, 1152, 1168)


def _even_proj_kernel(x_ref, g_ref, w_ref, wt_ref, wa2_ref, ba2_ref,
                      gq_ref, gk_ref, gv_ref, la_ref, gr_ref, dk_ref, ik_ref,
                      dqt_ref, iqt_ref, dvt_ref, iwt_ref):
    xn = _rms(x_ref[...], g_ref[...]).astype(BF16)

    def proj(a, b):
        return _dot(xn, w_ref[:, a:b])

    def proj_t(a, b):
        return _dot_nt(wt_ref[a:b, :], xn)

    gq_ref[...] = proj(_E_GQ, _E_GK)
    gk_ref[...] = proj(_E_GK, _E_GV)
    gv_ref[...] = proj(_E_GV, _E_GA).astype(BF16)
    ga = proj(_E_GA, _E_GR).astype(BF16)
    z = _dot(ga, wa2_ref[...]) + ba2_ref[...]
    la_ref[...] = jax.nn.log_sigmoid(z) * (1.0 / GLA_GATE_NORM)
    gr_ref[...] = proj(_E_GR, _E_DK)
    dk_ref[...] = proj(_E_DK, _E_IK).astype(BF16)
    ik_ref[...] = proj(_E_IK, _E_IK + IDX_DIM).astype(BF16)
    dqt_ref[...] = (proj_t(_T_DQ, _T_IQ) * (DSA_DH ** -0.5)).astype(BF16)
    iqt_ref[...] = proj_t(_T_IQ, _T_DV).astype(BF16)
    dvt = proj_t(_T_DV, _T_IW).astype(BF16)
    for c in range(dvt_ref.shape[0]):
        dvt_ref[c] = dvt[:, c * DSA_SK:(c + 1) * DSA_SK]
    iwt_ref[...] = proj_t(_T_IW, _T_END) * ((IDX_HEADS ** -0.5) * (IDX_DIM ** -0.5))


def _even_proj(x2, gain, w_in, wa2, ba2):
    m = x2.shape[0]
    tm = PROJ_TM
    s = [0, 256, 512, 1024, 1040, 1552, 2064, 2192, 2320, 2832, 2896, 2904]
    gq, gk, gv, ga, gr, dq, dk, dv, iq, ik, iw = [w_in[:, s[i]:s[i + 1]] for i in range(11)]
    zpad = lambda n: jnp.zeros((D_MODEL, n), w_in.dtype)
    w = jnp.concatenate([gq, gk, gv, ga, zpad(LANES - GLA_RANK), gr, dk, ik, zpad(LANES - IDX_DIM)],
                        axis=1).astype(BF16)
    wt = jnp.concatenate([dq, iq, dv, iw, zpad(_T_END - _T_IW - IDX_HEADS)], axis=1).T.astype(BF16)
    wa2p = jnp.concatenate(
        [wa2, jnp.zeros((LANES - GLA_RANK, wa2.shape[1]), wa2.dtype)], axis=0).astype(BF16)
    row = lambda n: pl.BlockSpec((tm, n), lambda i: (i, 0))
    col = lambda n: pl.BlockSpec((n, tm), lambda i: (0, i))
    n_iw = _T_END - _T_IW
    return pl.pallas_call(
        _even_proj_kernel,
        grid=(m // tm,),
        in_specs=[row(D_MODEL), _full((1, D_MODEL)), _full((D_MODEL, _E_END)),
                  _full((_T_END, D_MODEL)), _full((LANES, 256)), _full((1, 256))],
        out_specs=[row(256), row(256), row(512), row(256), row(512), row(DSA_DH), row(IDX_DIM),
                   col(512), col(512),
                   pl.BlockSpec((tm // DSA_SK, DSA_DH, DSA_SK), lambda i: (i, 0, 0)),
                   col(n_iw)],
        out_shape=[jax.ShapeDtypeStruct((m, 256), F32), jax.ShapeDtypeStruct((m, 256), F32),
                   jax.ShapeDtypeStruct((m, 512), BF16), jax.ShapeDtypeStruct((m, 256), F32),
                   jax.ShapeDtypeStruct((m, 512), F32), jax.ShapeDtypeStruct((m, DSA_DH), BF16),
                   jax.ShapeDtypeStruct((m, IDX_DIM), BF16),
                   jax.ShapeDtypeStruct((512, m), BF16), jax.ShapeDtypeStruct((512, m), BF16),
                   jax.ShapeDtypeStruct((m // DSA_SK, DSA_DH, DSA_SK), BF16),
                   jax.ShapeDtypeStruct((n_iw, m), F32)],
        compiler_params=_cparams("parallel"),
        name="even_proj",
    )(x2, gain.reshape(1, D_MODEL), w, wt, wa2p, ba2.reshape(1, 256))


def _gla_kernel(q_ref, k_ref, v_ref, la_ref, gr_ref, gn_ref, o_ref, st_ref, *, n_chunks):
    c_ = GLA_CHUNK

    @pl.when(pl.program_id(1) == 0)
    def _():
        st_ref[...] = jnp.zeros_like(st_ref)

    ri = lax.broadcasted_iota(jnp.int32, (c_, c_), 0)
    ci = lax.broadcasted_iota(jnp.int32, (c_, c_), 1)
    tri = ri >= ci
    ltri = jnp.where(tri, 1.0, 0.0).astype(BF16)
    gn = gn_ref[...]

    def chunk(c, carry):
        rows = pl.ds(pl.multiple_of(c * c_, c_), c_)
        g = la_ref[rows, :]
        g1 = g.astype(BF16)
        r1 = g - g1.astype(F32)
        g2 = r1.astype(BF16)
        g3 = (r1 - g2.astype(F32)).astype(BF16)
        cum = _dot(ltri, g1) + _dot(ltri, g2) + _dot(ltri, g3)
        last = cum[c_ - 1:c_, :]
        kk = k_ref[rows, :]
        q_t = (q_ref[rows, :] * jnp.exp(cum) * (GLA_DK ** -0.5)).astype(BF16)
        k_t = (kk * jnp.exp(-cum)).astype(BF16)
        k_e = (kk * jnp.exp(last - cum)).astype(BF16)
        dec = jnp.exp(last)
        for h in range(GLA_HEADS):
            ks = slice(h * GLA_DK, (h + 1) * GLA_DK)
            vs = slice(h * GLA_DV, (h + 1) * GLA_DV)
            v_h = v_ref[rows, vs]
            a = jnp.where(tri, _dot_nt(q_t[:, ks], k_t[:, ks]), 0.0).astype(BF16)
            st = st_ref[h]
            o = _dot(a, v_h) + _dot_nt(q_t[:, ks], st.astype(BF16))
            st_ref[h] = dec[:, ks] * st + _dot_tn(v_h, k_e[:, ks])
            y = _rms(o, gn) * _silu(gr_ref[rows, vs])
            o_ref[rows, vs] = y.astype(o_ref.dtype)
        return carry

    lax.fori_loop(0, n_chunks, chunk, 0)


def _gla(gq, gk, gv, la, gr, gnorm, b, t):
    tt = min(GLA_TT, t)
    nt = t // tt
    row = lambda n: pl.BlockSpec((tt, n), lambda i, j: (i * nt + j, 0))
    return pl.pallas_call(
        functools.partial(_gla_kernel, n_chunks=tt // GLA_CHUNK),
        grid=(b, nt),
        in_specs=[row(256), row(256), row(512), row(256), row(512), _full((1, GLA_DV))],
        out_specs=row(512),
        out_shape=jax.ShapeDtypeStruct((b * t, 512), BF16),
        scratch_shapes=[pltpu.VMEM((GLA_HEADS, GLA_DV, GLA_DK), F32)],
        compiler_params=_cparams("parallel", "arbitrary"),
        name="gla",
    )(gq, gk, gv, la, gr, gnorm.reshape(1, GLA_DV))


def _dsa_kernel(iqt_ref, dqt_ref, iwt_ref, ik_ref, dk_ref, dvt_ref, o_ref,
                s_ref, acc_ref, *, topk, t_len):
    tq, tk, sk = DSA_TQ, DSA_TK, DSA_SK
    n_sub = tk // sk
    qi = pl.program_id(1)
    q0 = qi * tq
    nkt = (q0 + tq + tk - 1) // tk
    neg_inf = -jnp.inf

    qpos = q0 + lax.broadcasted_iota(jnp.int32, (sk, tq), 1)
    krow = lax.broadcasted_iota(jnp.int32, (sk, tq), 0)
    krow_t = lax.broadcasted_iota(jnp.int32, (tk, tq), 0)

    def colsum(c):
        r = c.shape[0] // (4 * SUBLANES)
        part = jnp.sum(c.reshape(r, 4, SUBLANES, tq), axis=0)
        return jnp.sum(part, axis=0)

    def score_tile(kt, carry):
        rmax, rmin = carry
        for j in range(n_sub):
            k0 = pl.multiple_of(kt * tk, tk) + j * sk
            ik_t = ik_ref[pl.ds(k0, sk), :]
            sc = jnp.zeros((sk, tq), F32)
            for h in range(IDX_HEADS):
                lg = _dot(ik_t, iqt_ref[h * IDX_DIM:(h + 1) * IDX_DIM, :])
                sc = sc + iwt_ref[h:h + 1, :] * jnp.maximum(lg, 0.0)
            allowed = (k0 + krow) <= qpos
            s_ref[kt, j * sk:(j + 1) * sk, :] = jnp.where(allowed, sc, neg_inf)
            rmax = jnp.maximum(rmax, jnp.max(jnp.where(allowed, sc, neg_inf), axis=0, keepdims=True))
            rmin = jnp.minimum(rmin, jnp.min(jnp.where(allowed, sc, jnp.inf), axis=0, keepdims=True))
        return rmax, rmin

    rmax, rmin = lax.fori_loop(
        0, nkt, score_tile,
        (jnp.full((1, tq), neg_inf, F32), jnp.full((1, tq), jnp.inf, F32)))

    def count(ind_fn):
        def body(kt, acc):
            return acc + colsum(ind_fn(s_ref[kt], kt))
        acc = lax.fori_loop(0, nkt, body, jnp.zeros((SUBLANES, tq), F32))
        return jnp.sum(acc, axis=0, keepdims=True)

    kf = jnp.float32(topk)

    def bisect(_, carry):
        lo, hi = carry
        mid = lo + (hi - lo) * 0.5
        ge = count(lambda s, kt: jnp.where(s >= mid, 1.0, 0.0)) >= kf
        return jnp.where(ge, mid, lo), jnp.where(ge, hi, mid)

    lo, hi = lax.fori_loop(0, BISECT_ITERS, bisect, (rmin, rmax))

    def snap(kt, acc):
        s = s_ref[kt]
        c = jnp.where(s >= lo, s, jnp.inf)
        return jnp.minimum(acc, jnp.min(c.reshape(tk // SUBLANES, SUBLANES, tq), axis=0))

    thr = jnp.min(lax.fori_loop(0, nkt, snap, jnp.full((SUBLANES, tq), jnp.inf, F32)),
                  axis=0, keepdims=True)
    n_gt = count(lambda s, kt: jnp.where(s > thr, 1.0, 0.0))
    n_eq = count(lambda s, kt: jnp.where(s == thr, 1.0, 0.0))
    need = kf - n_gt

    def tie_limit():
        def ibisect(_, carry):
            ilo, ihi = carry
            imid = jnp.right_shift(ilo + ihi, 1)
            c = count(lambda s, kt: jnp.where(
                s == thr, jnp.where((kt * tk + krow_t) <= imid, 1.0, 0.0), 0.0))
            ge = c >= need
            return jnp.where(ge, ilo, imid), jnp.where(ge, imid, ihi)
        ilo0 = jnp.full((1, tq), -1, jnp.int32)
        ihi0 = jnp.full((1, tq), t_len - 1, jnp.int32)
        n_it = max(1, int(t_len - 1).bit_length() + 1)
        return lax.fori_loop(0, n_it, ibisect, (ilo0, ihi0))[1]

    jlim = lax.cond(jnp.max(n_eq - need) > 0.0, tie_limit,
                    lambda: jnp.full((1, tq), t_len, jnp.int32))

    acc_ref[...] = jnp.zeros(acc_ref.shape, F32)

    def attend(kt, carry):
        ms, ls = carry
        ms, ls = list(ms), list(ls)
        for j in range(n_sub):
            k0 = pl.multiple_of(kt * tk, tk) + j * sk
            s = s_ref[kt, j * sk:(j + 1) * sk, :]
            keep_tie = jnp.where((k0 + krow) <= jlim, 0.0, neg_inf)
            bias = jnp.where(s > thr, 0.0, jnp.where(s == thr, keep_tie, neg_inf))
            k_t = dk_ref[pl.ds(k0, sk), :]
            vt_t = dvt_ref[kt * n_sub + j]
            for h in range(DSA_HEADS):
                sh = _dot(k_t, dqt_ref[h * DSA_DH:(h + 1) * DSA_DH, :]) + bias
                m_new = jnp.maximum(ms[h], jnp.max(sh, axis=0, keepdims=True))
                m_safe = jnp.where(m_new == neg_inf, 0.0, m_new)
                alpha = jnp.exp(ms[h] - m_safe)
                p = jnp.exp(sh - m_safe)
                ls[h] = alpha * ls[h] + jnp.sum(p, axis=0, keepdims=True)
                acc_ref[h] = alpha * acc_ref[h] + _dot(vt_t, p.astype(BF16))
                ms[h] = m_new
        return tuple(ms), tuple(ls)

    m0 = tuple(jnp.full((1, tq), neg_inf, F32) for _ in range(DSA_HEADS))
    l0 = tuple(jnp.zeros((1, tq), F32) for _ in range(DSA_HEADS))
    _, ls = lax.fori_loop(0, nkt, attend, (m0, l0))
    for h in range(DSA_HEADS):
        o_ref[:, h * DSA_DH:(h + 1) * DSA_DH] = (acc_ref[h] / ls[h]).T.astype(o_ref.dtype)


def _dsa(iqt, dqt, iwt, ik, dk, dvt, b, t):
    tq, tk, sk = DSA_TQ, DSA_TK, DSA_SK
    nq = t // tq
    topk = min(TOPK_MAX, t // 4)
    qcol = lambda n: pl.BlockSpec((n, tq), lambda i, j: (0, i * nq + j))
    krow = lambda n: pl.BlockSpec((t, n), lambda i, j: (i, 0))
    return pl.pallas_call(
        functools.partial(_dsa_kernel, topk=topk, t_len=t),
        grid=(b, nq),
        in_specs=[qcol(512), qcol(512), qcol(iwt.shape[0]), krow(IDX_DIM), krow(DSA_DH),
                  pl.BlockSpec((t // sk, DSA_DH, sk), lambda i, j: (i, 0, 0))],
        out_specs=pl.BlockSpec((tq, 512), lambda i, j: (i * nq + j, 0)),
        out_shape=jax.ShapeDtypeStruct((b * t, 512), BF16),
        scratch_shapes=[
            pltpu.VMEM((t // tk, tk, tq), F32),
            pltpu.VMEM((DSA_HEADS, DSA_DH, tq), F32),
        ],
        compiler_params=_cparams("parallel", "arbitrary"),
        name="dsa",
    )(iqt, dqt, iwt, ik, dk, dvt)


def _out_proj_kernel(*refs, n_in):
    res_ref = refs[0]
    a_refs = refs[1:1 + n_in]
    w_refs = refs[1 + n_in:1 + 2 * n_in]
    o_ref = refs[1 + 2 * n_in]
    acc = res_ref[...]
    for a, w in zip(a_refs, w_refs):
        acc = acc + _dot(a[...], w[...])
    o_ref[...] = acc


def _out_proj(res, acts, ws):
    m = res.shape[0]
    tm = PROJ_TM
    row = lambda n: pl.BlockSpec((tm, n), lambda i: (i, 0))
    return pl.pallas_call(
        functools.partial(_out_proj_kernel, n_in=len(acts)),
        grid=(m // tm,),
        in_specs=[row(D_MODEL)] + [row(a.shape[1]) for a in acts] + [_full(w.shape) for w in ws],
        out_specs=row(D_MODEL),
        out_shape=jax.ShapeDtypeStruct((m, D_MODEL), F32),
        compiler_params=_cparams("parallel"),
        name="out_proj",
    )(res, *acts, *ws)


def _ffn_kernel(h_ref, g_ref, wg_ref, wu_ref, wd_ref, fg_ref, o_ref, xn_ref, acc_ref, *, final_norm):
    f = pl.program_id(1)

    @pl.when(f == 0)
    def _():
        xn_ref[...] = _rms(h_ref[...], g_ref[...]).astype(BF16)
        acc_ref[...] = h_ref[...]

    xn = xn_ref[...]
    a = _silu(_dot(xn, wg_ref[...])) * _dot(xn, wu_ref[...])
    acc_ref[...] += _dot(a.astype(BF16), wd_ref[...])

    @pl.when(f == pl.num_programs(1) - 1)
    def _():
        y = acc_ref[...]
        o_ref[...] = _rms(y, fg_ref[...]) if final_norm else y


def _ffn(h, gain, wg, wu, wd, final_gain=None):
    m = h.shape[0]
    tm, tf = FFN_TM, FFN_TF
    fg = jnp.ones((1, D_MODEL), F32) if final_gain is None else final_gain.reshape(1, D_MODEL)
    return pl.pallas_call(
        functools.partial(_ffn_kernel, final_norm=final_gain is not None),
        grid=(m // tm, D_FF // tf),
        in_specs=[pl.BlockSpec((tm, D_MODEL), lambda i, j: (i, 0)),
                  _full((1, D_MODEL)),
                  pl.BlockSpec((D_MODEL, tf), lambda i, j: (0, j)),
                  pl.BlockSpec((D_MODEL, tf), lambda i, j: (0, j)),
                  pl.BlockSpec((tf, D_MODEL), lambda i, j: (j, 0)),
                  _full((1, D_MODEL))],
        out_specs=pl.BlockSpec((tm, D_MODEL), lambda i, j: (i, 0)),
        out_shape=jax.ShapeDtypeStruct((m, D_MODEL), F32),
        scratch_shapes=[pltpu.VMEM((tm, D_MODEL), BF16), pltpu.VMEM((tm, D_MODEL), F32)],
        compiler_params=_cparams("parallel", "arbitrary"),
        name="ffn",
    )(h, gain.reshape(1, D_MODEL), wg.astype(BF16), wu.astype(BF16), wd.astype(BF16), fg)


def _odd_proj_kernel(x_ref, g_ref, w_ref, cos_ref, sin_ref, q_ref, k_ref, v_ref, gate_ref):
    xn = _rms(x_ref[...], g_ref[...]).astype(BF16)
    cos = cos_ref[...]
    sin = sin_ref[...]
    half = RET_DK // 2

    def rot(z, out_ref, scale):
        for h in range(RET_HEADS):
            x1 = z[:, h * RET_DK:h * RET_DK + half]
            x2 = z[:, h * RET_DK + half:(h + 1) * RET_DK]
            out_ref[:, h * RET_DK:h * RET_DK + half] = ((x1 * cos - x2 * sin) * scale).astype(BF16)
            out_ref[:, h * RET_DK + half:(h + 1) * RET_DK] = ((x2 * cos + x1 * sin) * scale).astype(BF16)

    nq = RET_HEADS * RET_DK
    nv = RET_HEADS * RET_DV
    rot(_dot(xn, w_ref[:, 0:nq]), q_ref, 1.0)
    rot(_dot(xn, w_ref[:, nq:2 * nq]), k_ref, RET_DK ** -0.5)
    v_ref[...] = _dot(xn, w_ref[:, 2 * nq:2 * nq + nv]).astype(BF16)
    gate_ref[...] = _dot(xn, w_ref[:, 2 * nq + nv:2 * nq + 2 * nv])


def _odd_proj(h, gain, w_in, t):
    m = h.shape[0]
    tm = PROJ_TM
    nt = t // tm
    half = RET_DK // 2
    inv = ROPE_BASE ** (-jnp.arange(half, dtype=F32) / half)
    ang = jnp.arange(t, dtype=jnp.int32).astype(F32)[:, None] * inv[None, :]
    row = lambda n: pl.BlockSpec((tm, n), lambda i: (i, 0))
    pos = pl.BlockSpec((tm, half), lambda i: (i % nt, 0))
    nq = RET_HEADS * RET_DK
    nv = RET_HEADS * RET_DV
    return pl.pallas_call(
        _odd_proj_kernel,
        grid=(m // tm,),
        in_specs=[row(D_MODEL), _full((1, D_MODEL)), _full(w_in.shape), pos, pos],
        out_specs=[row(nq), row(nq), row(nv), row(nv)],
        out_shape=[jax.ShapeDtypeStruct((m, nq), BF16), jax.ShapeDtypeStruct((m, nq), BF16),
                   jax.ShapeDtypeStruct((m, nv), BF16), jax.ShapeDtypeStruct((m, nv), F32)],
        compiler_params=_cparams("parallel"),
        name="odd_proj",
    )(h, gain.reshape(1, D_MODEL), w_in.astype(BF16), jnp.cos(ang), jnp.sin(ang))


def _ret_kernel(dec_ref, q_ref, k_ref, v_ref, gate_ref, dm_ref, xi_ref, ze_ref, rn_ref,
                o_ref, st_ref, *, n_chunks):
    c_ = RET_CHUNK

    @pl.when(pl.program_id(1) == 0)
    def _():
        st_ref[...] = jnp.zeros_like(st_ref)

    rn = rn_ref[...]
    for c in range(n_chunks):
        rows = slice(c * c_, (c + 1) * c_)
        for h in range(RET_HEADS):
            ks = slice(h * RET_DK, (h + 1) * RET_DK)
            vs = slice(h * RET_DV, (h + 1) * RET_DV)
            q = q_ref[rows, ks]
            k = k_ref[rows, ks]
            v = v_ref[rows, vs]
            st = st_ref[h]
            inner = (_dot_nt(q, k) * dm_ref[h]).astype(BF16)
            o = _dot(inner, v) + xi_ref[h] * _dot(q, st.astype(BF16))
            kz = (k.astype(F32) * ze_ref[h]).astype(BF16)
            st_ref[h] = dec_ref[h] * st + _dot_tn(kz, v)
            y = _rms(o, rn) * _silu(gate_ref[rows, vs])
            o_ref[rows, vs] = y.astype(o_ref.dtype)


def _retention(q, k, v, gate, rnorm, b, t):
    c_ = min(RET_CHUNK, t)
    tt = min(RET_TT, t)
    nt = t // tt
    log_g = jnp.log1p(-jnp.exp2(-5.0 - jnp.arange(RET_HEADS, dtype=F32)))
    idx = jnp.arange(c_, dtype=F32)
    rel = idx[:, None] - idx[None, :]
    dmat = jnp.where(rel >= 0, jnp.exp(log_g[:, None, None] * jnp.maximum(rel, 0.0)), 0.0)
    xi = jnp.exp(log_g[:, None] * (idx[None, :] + 1.0))[:, :, None]
    zeta = jnp.exp(log_g[:, None] * (c_ - 1.0 - idx[None, :]))[:, :, None]
    decay_c = jnp.exp(log_g * c_)
    nq = RET_HEADS * RET_DK
    nv = RET_HEADS * RET_DV
    row = lambda n: pl.BlockSpec((tt, n), lambda i, j: (i * nt + j, 0))
    return pl.pallas_call(
        functools.partial(_ret_kernel, n_chunks=tt // c_),
        grid=(b, nt),
        in_specs=[pl.BlockSpec(memory_space=pltpu.SMEM),
                  row(nq), row(nq), row(nv), row(nv),
                  _full((RET_HEADS, c_, c_)), _full((RET_HEADS, c_, 1)), _full((RET_HEADS, c_, 1)),
                  _full((1, RET_DV))],
        out_specs=row(nv),
        out_shape=jax.ShapeDtypeStruct((b * t, nv), BF16),
        scratch_shapes=[pltpu.VMEM((RET_HEADS, RET_DK, RET_DV), F32)],
        compiler_params=_cparams("parallel", "arbitrary"),
        name="retention",
    )(decay_c, q, k, v, gate, dmat, xi, zeta, rnorm.reshape(1, RET_DV))


def kernel(x, even_attn_norm, even_w_in, even_gla_wa2, even_gla_ba2, even_gla_norm, even_w_out,
           odd_attn_norm, odd_w_in, odd_ret_norm, odd_w_out,
           ffn_norm, ffn_w_gate, ffn_w_up, ffn_w_down, final_norm):
    b, t, d = x.shape
    h = x.reshape(b * t, d)

    gq, gk, gv, la, gr, dk, ik, dqt, iqt, dvt, iwt = _even_proj(
        h, even_attn_norm[0], even_w_in[0], even_gla_wa2[0], even_gla_ba2[0])
    o_gla = _gla(gq, gk, gv, la, gr, even_gla_norm[0], b, t)
    o_dsa = _dsa(iqt, dqt, iwt, ik, dk, dvt, b, t)
    w_out = even_w_out[0].astype(BF16)
    n_gla = GLA_HEADS * GLA_DV
    h = _out_proj(h, [o_gla, o_dsa], [w_out[:n_gla], w_out[n_gla:]])
    h = _ffn(h, ffn_norm[0], ffn_w_gate[0], ffn_w_up[0], ffn_w_down[0])

    q, k, v, gate = _odd_proj(h, odd_attn_norm[0], odd_w_in[0], t)
    o_ret = _retention(q, k, v, gate, odd_ret_norm[0], b, t)
    h = _out_proj(h, [o_ret], [odd_w_out[0].astype(BF16)])
    h = _ffn(h, ffn_norm[1], ffn_w_gate[1], ffn_w_up[1], ffn_w_down[1], final_gain=final_norm)
    return h.reshape(b, t, d)
```

```python
import functools

import jax
import jax.numpy as jnp
from jax import lax
from jax.experimental import pallas as pl
from jax.experimental.pallas import tpu as pltpu

F32 = jnp.float32
BF16 = jnp.bfloat16

D_MODEL = 1024
EPS = 1e-6
GLA_HEADS = 4
GLA_DV = 128
GLA_DK = 64
GLA_RANK = 16
GLA_GATE_NORM = 16.0
GLA_CHUNK = 64
DSA_HEADS = 4
DSA_DH = 128
IDX_HEADS = 8
IDX_DIM = 64
TOPK_MAX = 256
RET_HEADS = 4
RET_DK = 256
RET_DV = 512
ROPE_BASE = 10000.0
D_FF = 2816

LANES = 128
SUBLANES = 8
VMEM_LIMIT = 56 * 1024 * 1024

PROJ_TM = 512
FFN_TM = 512
FFN_TF = 1408
GLA_TT = 512
RET_CHUNK = 256
RET_TT = 512
DSA_TQ = 256
DSA_TK = 256
DSA_SK = 128
BISECT_ITERS = 16


def _cparams(*sem):
    return pltpu.CompilerParams(dimension_semantics=sem, vmem_limit_bytes=VMEM_LIMIT)


def _dot(a, b):
    return jnp.dot(a, b, preferred_element_type=F32)


def _dot_nt(a, b):
    return lax.dot_general(a, b, (((1,), (1,)), ((), ())), preferred_element_type=F32)


def _dot_tn(a, b):
    return lax.dot_general(a, b, (((0,), (0,)), ((), ())), preferred_element_type=F32)


def _rms(x, g):
    return x * lax.rsqrt(jnp.mean(x * x, axis=-1, keepdims=True) + EPS) * g


def _silu(x):
    return x * jax.nn.sigmoid(x)


def _full(shape):
    return pl.BlockSpec(shape, lambda *_: (0,) * len(shape))


_E_GQ, _E_GK, _E_GV, _E_GA, _E_GR, _E_DK, _E_IK, _E_END = (0, 256, 512, 1024, 1152, 1664, 1792, 1920)
_T_DQ, _T_IQ, _T_DV, _T_IW, _T_END = (0, 512, 1024, 1152, 1168)


def _even_proj_kernel(x_ref, g_ref, w_ref, wt_ref, wa2_ref, ba2_ref,
                      gq_ref, gk_ref, gv_ref, la_ref, gr_ref, dk_ref, ik_ref,
                      dqt_ref, iqt_ref, dvt_ref, iwt_ref):
    xn = _rms(x_ref[...], g_ref[...]).astype(BF16)

    def proj(a, b):
        return _dot(xn, w_ref[:, a:b])

    def proj_t(a, b):
        return _dot_nt(wt_ref[a:b, :], xn)

    gq_ref[...] = proj(_E_GQ, _E_GK)
    gk_ref[...] = proj(_E_GK, _E_GV)
    gv_ref[...] = proj(_E_GV, _E_GA).astype(BF16)
    ga = proj(_E_GA, _E_GR).astype(BF16)
    z = _dot(ga, wa2_ref[...]) + ba2_ref[...]
    la_ref[...] = jax.nn.log_sigmoid(z) * (1.0 / GLA_GATE_NORM)
    gr_ref[...] = proj(_E_GR, _E_DK)
    dk_ref[...] = proj(_E_DK, _E_IK).astype(BF16)
    ik_ref[...] = proj(_E_IK, _E_IK + IDX_DIM).astype(BF16)
    dqt_ref[...] = (proj_t(_T_DQ, _T_IQ) * (DSA_DH ** -0.5)).astype(BF16)
    iqt_ref[...] = proj_t(_T_IQ, _T_DV).astype(BF16)
    dvt = proj_t(_T_DV, _T_IW).astype(BF16)
    for c in range(dvt_ref.shape[0]):
        dvt_ref[c] = dvt[:, c * DSA_SK:(c + 1) * DSA_SK]
    iwt_ref[...] = proj_t(_T_IW, _T_END) * ((IDX_HEADS ** -0.5) * (IDX_DIM ** -0.5))


def _even_proj(x2, gain, w_in, wa2, ba2):
    m = x2.shape[0]
    tm = PROJ_TM
    s = [0, 256, 512, 1024, 1040, 1552, 2064, 2192, 2320, 2832, 2896, 2904]
    gq, gk, gv, ga, gr, dq, dk, dv, iq, ik, iw = [w_in[:, s[i]:s[i + 1]] for i in range(11)]
    zpad = lambda n: jnp.zeros((D_MODEL, n), w_in.dtype)
    w = jnp.concatenate([gq, gk, gv, ga, zpad(LANES - GLA_RANK), gr, dk, ik, zpad(LANES - IDX_DIM)],
                        axis=1).astype(BF16)
    wt = jnp.concatenate([dq, iq, dv, iw, zpad(_T_END - _T_IW - IDX_HEADS)], axis=1).T.astype(BF16)
    wa2p = jnp.concatenate(
        [wa2, jnp.zeros((LANES - GLA_RANK, wa2.shape[1]), wa2.dtype)], axis=0).astype(BF16)
    row = lambda n: pl.BlockSpec((tm, n), lambda i: (i, 0))
    col = lambda n: pl.BlockSpec((n, tm), lambda i: (0, i))
    n_iw = _T_END - _T_IW
    return pl.pallas_call(
        _even_proj_kernel,
        grid=(m // tm,),
        in_specs=[row(D_MODEL), _full((1, D_MODEL)), _full((D_MODEL, _E_END)),
                  _full((_T_END, D_MODEL)), _full((LANES, 256)), _full((1, 256))],
        out_specs=[row(256), row(256), row(512), row(256), row(512), row(DSA_DH), row(IDX_DIM),
                   col(512), col(512),
                   pl.BlockSpec((tm // DSA_SK, DSA_DH, DSA_SK), lambda i: (i, 0, 0)),
                   col(n_iw)],
        out_shape=[jax.ShapeDtypeStruct((m, 256), F32), jax.ShapeDtypeStruct((m, 256), F32),
                   jax.ShapeDtypeStruct((m, 512), BF16), jax.ShapeDtypeStruct((m, 256), F32),
                   jax.ShapeDtypeStruct((m, 512), F32), jax.ShapeDtypeStruct((m, DSA_DH), BF16),
                   jax.ShapeDtypeStruct((m, IDX_DIM), BF16),
                   jax.ShapeDtypeStruct((512, m), BF16), jax.ShapeDtypeStruct((512, m), BF16),
                   jax.ShapeDtypeStruct((m // DSA_SK, DSA_DH, DSA_SK), BF16),
                   jax.ShapeDtypeStruct((n_iw, m), F32)],
        compiler_params=_cparams("parallel"),
        name="even_proj",
    )(x2, gain.reshape(1, D_MODEL), w, wt, wa2p, ba2.reshape(1, 256))


def _gla_kernel(q_ref, k_ref, v_ref, la_ref, gr_ref, gn_ref, o_ref, st_ref, *, n_chunks):
    c_ = GLA_CHUNK

    @pl.when(pl.program_id(1) == 0)
    def _():
        st_ref[...] = jnp.zeros_like(st_ref)

    ri = lax.broadcasted_iota(jnp.int32, (c_, c_), 0)
    ci = lax.broadcasted_iota(jnp.int32, (c_, c_), 1)
    tri = ri >= ci
    ltri = jnp.where(tri, 1.0, 0.0).astype(BF16)
    gn = gn_ref[...]

    def chunk(c, carry):
        rows = pl.ds(pl.multiple_of(c * c_, c_), c_)
        g = la_ref[rows, :]
        g1 = g.astype(BF16)
        r1 = g - g1.astype(F32)
        g2 = r1.astype(BF16)
        g3 = (r1 - g2.astype(F32)).astype(BF16)
        cum = _dot(ltri, g1) + _dot(ltri, g2) + _dot(ltri, g3)
        last = cum[c_ - 1:c_, :]
        kk = k_ref[rows, :]
        q_t = (q_ref[rows, :] * jnp.exp(cum) * (GLA_DK ** -0.5)).astype(BF16)
        k_t = (kk * jnp.exp(-cum)).astype(BF16)
        k_e = (kk * jnp.exp(last - cum)).astype(BF16)
        dec = jnp.exp(last)
        for h in range(GLA_HEADS):
            ks = slice(h * GLA_DK, (h + 1) * GLA_DK)
            vs = slice(h * GLA_DV, (h + 1) * GLA_DV)
            v_h = v_ref[rows, vs]
            a = jnp.where(tri, _dot_nt(q_t[:, ks], k_t[:, ks]), 0.0).astype(BF16)
            st = st_ref[h]
            o = _dot(a, v_h) + _dot_nt(q_t[:, ks], st.astype(BF16))
            st_ref[h] = dec[:, ks] * st + _dot_tn(v_h, k_e[:, ks])
            y = _rms(o, gn) * _silu(gr_ref[rows, vs])
            o_ref[rows, vs] = y.astype(o_ref.dtype)
        return carry

    lax.fori_loop(0, n_chunks, chunk, 0)


def _gla(gq, gk, gv, la, gr, gnorm, b, t):
    tt = min(GLA_TT, t)
    nt = t // tt
    row = lambda n: pl.BlockSpec((tt, n), lambda i, j: (i * nt + j, 0))
    return pl.pallas_call(
        functools.partial(_gla_kernel, n_chunks=tt // GLA_CHUNK),
        grid=(b, nt),
        in_specs=[row(256), row(256), row(512), row(256), row(512), _full((1, GLA_DV))],
        out_specs=row(512),
        out_shape=jax.ShapeDtypeStruct((b * t, 512), BF16),
        scratch_shapes=[pltpu.VMEM((GLA_HEADS, GLA_DV, GLA_DK), F32)],
        compiler_params=_cparams("parallel", "arbitrary"),
        name="gla",
    )(gq, gk, gv, la, gr, gnorm.reshape(1, GLA_DV))


def _dsa_kernel(iqt_ref, dqt_ref, iwt_ref, ik_ref, dk_ref, dvt_ref, o_ref,
                s_ref, acc_ref, *, topk, t_len):
    tq, tk, sk = DSA_TQ, DSA_TK, DSA_SK
    n_sub = tk // sk
    qi = pl.program_id(1)
    q0 = qi * tq
    nkt = (q0 + tq + tk - 1) // tk
    neg_inf = -jnp.inf

    qpos = q0 + lax.broadcasted_iota(jnp.int32, (sk, tq), 1)
    krow = lax.broadcasted_iota(jnp.int32, (sk, tq), 0)
    krow_t = lax.broadcasted_iota(jnp.int32, (tk, tq), 0)

    def colsum(c):
        r = c.shape[0] // (4 * SUBLANES)
        part = jnp.sum(c.reshape(r, 4, SUBLANES, tq), axis=0)
        return jnp.sum(part, axis=0)

    def score_tile(kt, carry):
        rmax, rmin = carry
        for j in range(n_sub):
            k0 = pl.multiple_of(kt * tk, tk) + j * sk
            ik_t = ik_ref[pl.ds(k0, sk), :]
            sc = jnp.zeros((sk, tq), F32)
            for h in range(IDX_HEADS):
                lg = _dot(ik_t, iqt_ref[h * IDX_DIM:(h + 1) * IDX_DIM, :])
                sc = sc + iwt_ref[h:h + 1, :] * jnp.maximum(lg, 0.0)
            allowed = (k0 + krow) <= qpos
            s_ref[kt, j * sk:(j + 1) * sk, :] = jnp.where(allowed, sc, neg_inf)
            rmax = jnp.maximum(rmax, jnp.max(jnp.where(allowed, sc, neg_inf), axis=0, keepdims=True))
            rmin = jnp.minimum(rmin, jnp.min(jnp.where(allowed, sc, jnp.inf), axis=0, keepdims=True))
        return rmax, rmin

    rmax, rmin = lax.fori_loop(
        0, nkt, score_tile,
        (jnp.full((1, tq), neg_inf, F32), jnp.full((1, tq), jnp.inf, F32)))

    def count(ind_fn):
        def body(kt, acc):
            return acc + colsum(ind_fn(s_ref[kt], kt))
        acc = lax.fori_loop(0, nkt, body, jnp.zeros((SUBLANES, tq), F32))
        return jnp.sum(acc, axis=0, keepdims=True)

    kf = jnp.float32(topk)
    n_allowed = (qpos[0:1, :] + 1).astype(F32)

    def bisect(_, carry):
        lo, hi, c_lo = carry
        mid = lo + (hi - lo) * 0.5
        c = count(lambda s, kt: jnp.where(s >= mid, 1.0, 0.0))
        ge = c >= kf
        return jnp.where(ge, mid, lo), jnp.where(ge, hi, mid), jnp.where(ge, c, c_lo)

    lo, _, c_lo = lax.fori_loop(0, BISECT_ITERS, bisect, (rmin, rmax, n_allowed))

    def colmin(c):
        return jnp.min(c.reshape(tk // SUBLANES, SUBLANES, tq), axis=0)

    def snap(kt, acc):
        s = s_ref[kt]
        return jnp.minimum(acc, colmin(jnp.where(s >= lo, s, jnp.inf)))

    v0 = jnp.min(lax.fori_loop(0, nkt, snap, jnp.full((SUBLANES, tq), jnp.inf, F32)),
                 axis=0, keepdims=True)

    def above(v):
        def body(kt, carry):
            cnt, mn = carry
            s = s_ref[kt]
            gt = s > v
            return (cnt + colsum(jnp.where(gt, 1.0, 0.0)),
                    jnp.minimum(mn, colmin(jnp.where(gt, s, jnp.inf))))
        cnt, mn = lax.fori_loop(0, nkt, body, (jnp.zeros((SUBLANES, tq), F32),
                                                jnp.full((SUBLANES, tq), jnp.inf, F32)))
        return jnp.sum(cnt, axis=0, keepdims=True), jnp.min(mn, axis=0, keepdims=True)

    def step_up(carry):
        v, f_v, _, _ = carry
        n_gt, v_next = above(v)
        fin = n_gt < kf
        pending = jnp.max(jnp.where(fin, 0.0, 1.0))
        return jnp.where(fin, v, v_next), jnp.where(fin, f_v, n_gt), n_gt, pending

    thr, f_thr, n_gt, _ = lax.while_loop(
        lambda carry: carry[3] > 0.0, step_up, (v0, c_lo, jnp.zeros((1, tq), F32), jnp.float32(1.0)))
    n_eq = f_thr - n_gt
    need = kf - n_gt

    def tie_limit():
        def ibisect(_, carry):
            ilo, ihi = carry
            imid = jnp.right_shift(ilo + ihi, 1)
            c = count(lambda s, kt: jnp.where(
                s == thr, jnp.where((kt * tk + krow_t) <= imid, 1.0, 0.0), 0.0))
            ge = c >= need
            return jnp.where(ge, ilo, imid), jnp.where(ge, imid, ihi)
        ilo0 = jnp.full((1, tq), -1, jnp.int32)
        ihi0 = jnp.full((1, tq), t_len - 1, jnp.int32)
        n_it = max(1, int(t_len - 1).bit_length() + 1)
        return lax.fori_loop(0, n_it, ibisect, (ilo0, ihi0))[1]

    jlim = lax.cond(jnp.max(n_eq - need) > 0.0, tie_limit,
                    lambda: jnp.full((1, tq), t_len, jnp.int32))

    acc_ref[...] = jnp.zeros(acc_ref.shape, F32)

    def attend(kt, carry):
        ms, ls = carry
        ms, ls = list(ms), list(ls)
        for j in range(n_sub):
            k0 = pl.multiple_of(kt * tk, tk) + j * sk
            s = s_ref[kt, j * sk:(j + 1) * sk, :]
            keep_tie = jnp.where((k0 + krow) <= jlim, 0.0, neg_inf)
            bias = jnp.where(s > thr, 0.0, jnp.where(s == thr, keep_tie, neg_inf))
            k_t = dk_ref[pl.ds(k0, sk), :]
            vt_t = dvt_ref[kt * n_sub + j]
            for h in range(DSA_HEADS):
                sh = _dot(k_t, dqt_ref[h * DSA_DH:(h + 1) * DSA_DH, :]) + bias
                m_new = jnp.maximum(ms[h], jnp.max(sh, axis=0, keepdims=True))
                m_safe = jnp.where(m_new == neg_inf, 0.0, m_new)
                alpha = jnp.exp(ms[h] - m_safe)
                p = jnp.exp(sh - m_safe)
                ls[h] = alpha * ls[h] + jnp.sum(p, axis=0, keepdims=True)
                acc_ref[h] = alpha * acc_ref[h] + _dot(vt_t, p.astype(BF16))
                ms[h] = m_new
        return tuple(ms), tuple(ls)

    m0 = tuple(jnp.full((1, tq), neg_inf, F32) for _ in range(DSA_HEADS))
    l0 = tuple(jnp.zeros((1, tq), F32) for _ in range(DSA_HEADS))
    _, ls = lax.fori_loop(0, nkt, attend, (m0, l0))
    for h in range(DSA_HEADS):
        o_ref[:, h * DSA_DH:(h + 1) * DSA_DH] = (acc_ref[h] / ls[h]).T.astype(o_ref.dtype)


def _dsa(iqt, dqt, iwt, ik, dk, dvt, b, t):
    tq, tk, sk = DSA_TQ, DSA_TK, DSA_SK
    nq = t // tq
    topk = min(TOPK_MAX, t // 4)
    qcol = lambda n: pl.BlockSpec((n, tq), lambda i, j: (0, i * nq + j))
    krow = lambda n: pl.BlockSpec((t, n), lambda i, j: (i, 0))
    return pl.pallas_call(
        functools.partial(_dsa_kernel, topk=topk, t_len=t),
        grid=(b, nq),
        in_specs=[qcol(512), qcol(512), qcol(iwt.shape[0]), krow(IDX_DIM), krow(DSA_DH),
                  pl.BlockSpec((t // sk, DSA_DH, sk), lambda i, j: (i, 0, 0))],
        out_specs=pl.BlockSpec((tq, 512), lambda i, j: (i * nq + j, 0)),
        out_shape=jax.ShapeDtypeStruct((b * t, 512), BF16),
        scratch_shapes=[
            pltpu.VMEM((t // tk, tk, tq), F32),
            pltpu.VMEM((DSA_HEADS, DSA_DH, tq), F32),
        ],
        compiler_params=_cparams("parallel", "arbitrary"),
        name="dsa",
    )(iqt, dqt, iwt, ik, dk, dvt)


def _out_proj_kernel(*refs, n_in):
    res_ref = refs[0]
    a_refs = refs[1:1 + n_in]
    w_refs = refs[1 + n_in:1 + 2 * n_in]
    o_ref = refs[1 + 2 * n_in]
    acc = res_ref[...]
    for a, w in zip(a_refs, w_refs):
        acc = acc + _dot(a[...], w[...])
    o_ref[...] = acc


def _out_proj(res, acts, ws):
    m = res.shape[0]
    tm = PROJ_TM
    row = lambda n: pl.BlockSpec((tm, n), lambda i: (i, 0))
    return pl.pallas_call(
        functools.partial(_out_proj_kernel, n_in=len(acts)),
        grid=(m // tm,),
        in_specs=[row(D_MODEL)] + [row(a.shape[1]) for a in acts] + [_full(w.shape) for w in ws],
        out_specs=row(D_MODEL),
        out_shape=jax.ShapeDtypeStruct((m, D_MODEL), F32),
        compiler_params=_cparams("parallel"),
        name="out_proj",
    )(res, *acts, *ws)


def _ffn_kernel(h_ref, g_ref, wg_ref, wu_ref, wd_ref, fg_ref, o_ref, xn_ref, acc_ref, *, final_norm):
    f = pl.program_id(1)

    @pl.when(f == 0)
    def _():
        xn_ref[...] = _rms(h_ref[...], g_ref[...]).astype(BF16)
        acc_ref[...] = h_ref[...]

    xn = xn_ref[...]
    a = _silu(_dot(xn, wg_ref[...])) * _dot(xn, wu_ref[...])
    acc_ref[...] += _dot(a.astype(BF16), wd_ref[...])

    @pl.when(f == pl.num_programs(1) - 1)
    def _():
        y = acc_ref[...]
        o_ref[...] = _rms(y, fg_ref[...]) if final_norm else y


def _ffn(h, gain, wg, wu, wd, final_gain=None):
    m = h.shape[0]
    tm, tf = FFN_TM, FFN_TF
    fg = jnp.ones((1, D_MODEL), F32) if final_gain is None else final_gain.reshape(1, D_MODEL)
    return pl.pallas_call(
        functools.partial(_ffn_kernel, final_norm=final_gain is not None),
        grid=(m // tm, D_FF // tf),
        in_specs=[pl.BlockSpec((tm, D_MODEL), lambda i, j: (i, 0)),
                  _full((1, D_MODEL)),
                  pl.BlockSpec((D_MODEL, tf), lambda i, j: (0, j)),
                  pl.BlockSpec((D_MODEL, tf), lambda i, j: (0, j)),
                  pl.BlockSpec((tf, D_MODEL), lambda i, j: (j, 0)),
                  _full((1, D_MODEL))],
        out_specs=pl.BlockSpec((tm, D_MODEL), lambda i, j: (i, 0)),
        out_shape=jax.ShapeDtypeStruct((m, D_MODEL), F32),
        scratch_shapes=[pltpu.VMEM((tm, D_MODEL), BF16), pltpu.VMEM((tm, D_MODEL), F32)],
        compiler_params=_cparams("parallel", "arbitrary"),
        name="ffn",
    )(h, gain.reshape(1, D_MODEL), wg.astype(BF16), wu.astype(BF16), wd.astype(BF16), fg)


def _odd_proj_kernel(x_ref, g_ref, w_ref, cos_ref, sin_ref, q_ref, k_ref, v_ref, gate_ref):
    xn = _rms(x_ref[...], g_ref[...]).astype(BF16)
    cos = cos_ref[...]
    sin = sin_ref[...]
    half = RET_DK // 2

    def rot(z, out_ref, scale):
        for h in range(RET_HEADS):
            x1 = z[:, h * RET_DK:h * RET_DK + half]
            x2 = z[:, h * RET_DK + half:(h + 1) * RET_DK]
            out_ref[:, h * RET_DK:h * RET_DK + half] = ((x1 * cos - x2 * sin) * scale).astype(BF16)
            out_ref[:, h * RET_DK + half:(h + 1) * RET_DK] = ((x2 * cos + x1 * sin) * scale).astype(BF16)

    nq = RET_HEADS * RET_DK
    nv = RET_HEADS * RET_DV
    rot(_dot(xn, w_ref[:, 0:nq]), q_ref, 1.0)
    rot(_dot(xn, w_ref[:, nq:2 * nq]), k_ref, RET_DK ** -0.5)
    v_ref[...] = _dot(xn, w_ref[:, 2 * nq:2 * nq + nv]).astype(BF16)
    gate_ref[...] = _dot(xn, w_ref[:, 2 * nq + nv:2 * nq + 2 * nv])


def _odd_proj(h, gain, w_in, t):
    m = h.shape[0]
    tm = PROJ_TM
    nt = t // tm
    half = RET_DK // 2
    inv = ROPE_BASE ** (-jnp.arange(half, dtype=F32) / half)
    ang = jnp.arange(t, dtype=jnp.int32).astype(F32)[:, None] * inv[None, :]
    row = lambda n: pl.BlockSpec((tm, n), lambda i: (i, 0))
    pos = pl.BlockSpec((tm, half), lambda i: (i % nt, 0))
    nq = RET_HEADS * RET_DK
    nv = RET_HEADS * RET_DV
    return pl.pallas_call(
        _odd_proj_kernel,
        grid=(m // tm,),
        in_specs=[row(D_MODEL), _full((1, D_MODEL)), _full(w_in.shape), pos, pos],
        out_specs=[row(nq), row(nq), row(nv), row(nv)],
        out_shape=[jax.ShapeDtypeStruct((m, nq), BF16), jax.ShapeDtypeStruct((m, nq), BF16),
                   jax.ShapeDtypeStruct((m, nv), BF16), jax.ShapeDtypeStruct((m, nv), F32)],
        compiler_params=_cparams("parallel"),
        name="odd_proj",
    )(h, gain.reshape(1, D_MODEL), w_in.astype(BF16), jnp.cos(ang), jnp.sin(ang))


def _ret_kernel(dec_ref, q_ref, k_ref, v_ref, gate_ref, dm_ref, xi_ref, ze_ref, rn_ref,
                o_ref, st_ref, *, n_chunks):
    c_ = RET_CHUNK

    @pl.when(pl.program_id(1) == 0)
    def _():
        st_ref[...] = jnp.zeros_like(st_ref)

    rn = rn_ref[...]
    for c in range(n_chunks):
        rows = slice(c * c_, (c + 1) * c_)
        for h in range(RET_HEADS):
            ks = slice(h * RET_DK, (h + 1) * RET_DK)
            vs = slice(h * RET_DV, (h + 1) * RET_DV)
            q = q_ref[rows, ks]
            k = k_ref[rows, ks]
            v = v_ref[rows, vs]
            st = st_ref[h]
            inner = (_dot_nt(q, k) * dm_ref[h]).astype(BF16)
            o = _dot(inner, v) + xi_ref[h] * _dot(q, st.astype(BF16))
            kz = (k.astype(F32) * ze_ref[h]).astype(BF16)
            st_ref[h] = dec_ref[h] * st + _dot_tn(kz, v)
            y = _rms(o, rn) * _silu(gate_ref[rows, vs])
            o_ref[rows, vs] = y.astype(o_ref.dtype)


def _retention(q, k, v, gate, rnorm, b, t):
    c_ = min(RET_CHUNK, t)
    tt = min(RET_TT, t)
    nt = t // tt
    log_g = jnp.log1p(-jnp.exp2(-5.0 - jnp.arange(RET_HEADS, dtype=F32)))
    idx = jnp.arange(c_, dtype=F32)
    rel = idx[:, None] - idx[None, :]
    dmat = jnp.where(rel >= 0, jnp.exp(log_g[:, None, None] * jnp.maximum(rel, 0.0)), 0.0)
    xi = jnp.exp(log_g[:, None] * (idx[None, :] + 1.0))[:, :, None]
    zeta = jnp.exp(log_g[:, None] * (c_ - 1.0 - idx[None, :]))[:, :, None]
    decay_c = jnp.exp(log_g * c_)
    nq = RET_HEADS * RET_DK
    nv = RET_HEADS * RET_DV
    row = lambda n: pl.BlockSpec((tt, n), lambda i, j: (i * nt + j, 0))
    return pl.pallas_call(
        functools.partial(_ret_kernel, n_chunks=tt // c_),
        grid=(b, nt),
        in_specs=[pl.BlockSpec(memory_space=pltpu.SMEM),
                  row(nq), row(nq), row(nv), row(nv),
                  _full((RET_HEADS, c_, c_)), _full((RET_HEADS, c_, 1)), _full((RET_HEADS, c_, 1)),
                  _full((1, RET_DV))],
        out_specs=row(nv),
        out_shape=jax.ShapeDtypeStruct((b * t, nv), BF16),
        scratch_shapes=[pltpu.VMEM((RET_HEADS, RET_DK, RET_DV), F32)],
        compiler_params=_cparams("parallel", "arbitrary"),
        name="retention",
    )(decay_c, q, k, v, gate, dmat, xi, zeta, rnorm.reshape(1, RET_DV))


def kernel(x, even_attn_norm, even_w_in, even_gla_wa2, even_gla_ba2, even_gla_norm, even_w_out,
           odd_attn_norm, odd_w_in, odd_ret_norm, odd_w_out,
           ffn_norm, ffn_w_gate, ffn_w_up, ffn_w_down, final_norm):
    b, t, d = x.shape
    h = x.reshape(b * t, d)

    gq, gk, gv, la, gr, dk, ik, dqt, iqt, dvt, iwt = _even_proj(
        h, even_attn_norm[0], even_w_in[0], even_gla_wa2[0], even_gla_ba2[0])
    o_gla = _gla(gq, gk, gv, la, gr, even_gla_norm[0], b, t)
    o_dsa = _dsa(iqt, dqt, iwt, ik, dk, dvt, b, t)
    w_out = even_w_out[0].astype(BF16)
    n_gla = GLA_HEADS * GLA_DV
    h = _out_proj(h, [o_gla, o_dsa], [w_out[:n_gla], w_out[n_gla:]])
    h = _ffn(h, ffn_norm[0], ffn_w_gate[0], ffn_w_up[0], ffn_w_down[0])

    q, k, v, gate = _odd_proj(h, odd_attn_norm[0], odd_w_in[0], t)
    o_ret = _retention(q, k, v, gate, odd_ret_norm[0], b, t)
    h = _out_proj(h, [o_ret], [odd_w_out[0].astype(BF16)])
    h = _ffn(h, ffn_norm[1], ffn_w_gate[1], ffn_w_up[1], ffn_w_down[1], final_gain=final_norm)
    return h.reshape(b, t, d)
```

```python
import functools

import jax
import jax.numpy as jnp
from jax import lax
from jax.experimental import pallas as pl
from jax.experimental.pallas import tpu as pltpu

F32 = jnp.float32
BF16 = jnp.bfloat16

D_MODEL = 1024
EPS = 1e-6
GLA_HEADS = 4
GLA_DV = 128
GLA_DK = 64
GLA_RANK = 16
GLA_GATE_NORM = 16.0
GLA_CHUNK = 64
DSA_HEADS = 4
DSA_DH = 128
IDX_HEADS = 8
IDX_DIM = 64
TOPK_MAX = 256
RET_HEADS = 4
RET_DK = 256
RET_DV = 512
ROPE_BASE = 10000.0
D_FF = 2816

LANES = 128
SUBLANES = 8
VMEM_LIMIT = 56 * 1024 * 1024

PROJ_TM = 512
FFN_TM = 512
FFN_CHUNKS = ((0, 1024), (1024, 2048), (2048, 2816))
GLA_TT = 512
RET_CHUNK = 256
RET_TT = 512
DSA_TQ = 256
DSA_TK = 256
DSA_SK = 128
DSA_VROWS = DSA_DH + 16
LOG2E = 1.4426950408889634
BISECT_ITERS = 16


def _cparams(*sem):
    return pltpu.CompilerParams(dimension_semantics=sem, vmem_limit_bytes=VMEM_LIMIT)


def _dot(a, b):
    return jnp.dot(a, b, preferred_element_type=F32)


def _dot_nt(a, b):
    return lax.dot_general(a, b, (((1,), (1,)), ((), ())), preferred_element_type=F32)


def _dot_tn(a, b):
    return lax.dot_general(a, b, (((0,), (0,)), ((), ())), preferred_element_type=F32)


def _rms(x, g):
    return x * lax.rsqrt(jnp.mean(x * x, axis=-1, keepdims=True) + EPS) * g


def _silu(x):
    return x * jax.nn.sigmoid(x)


def _full(shape):
    return pl.BlockSpec(shape, lambda *_: (0,) * len(shape))


def _resident(shape):
    return pl.BlockSpec(shape, lambda *_: (0,) * len(shape), pipeline_mode=pl.Buffered(1))


_E_GQ, _E_GK, _E_GV, _E_GA, _E_GR, _E_DK, _E_IK, _E_END = (0, 256, 512, 1024, 1152, 1664, 1792, 1920)
_T_DQ, _T_IQ, _T_DV, _T_IW, _T_END = (0, 512, 1024, 1152, 1168)


def _even_proj_kernel(x_ref, g_ref, w_ref, wt_ref, wa2_ref, ba2_ref,
                      gq_ref, gk_ref, gv_ref, la_ref, gr_ref, dk_ref, ik_ref,
                      dqt_ref, iqt_ref, dvt_ref, iwt_ref):
    xn = _rms(x_ref[...], g_ref[...]).astype(BF16)

    def proj(a, b):
        return _dot(xn, w_ref[:, a:b])

    def proj_t(a, b):
        return _dot_nt(wt_ref[a:b, :], xn)

    gq_ref[...] = proj(_E_GQ, _E_GK)
    gk_ref[...] = proj(_E_GK, _E_GV)
    gv_ref[...] = proj(_E_GV, _E_GA).astype(BF16)
    ga = proj(_E_GA, _E_GR).astype(BF16)
    z = _dot(ga, wa2_ref[...]) + ba2_ref[...]
    la_ref[...] = jax.nn.log_sigmoid(z) * (1.0 / GLA_GATE_NORM)
    gr_ref[...] = proj(_E_GR, _E_DK)
    dk_ref[...] = proj(_E_DK, _E_IK).astype(BF16)
    ik_ref[...] = proj(_E_IK, _E_IK + IDX_DIM).astype(BF16)
    dqt_ref[...] = (proj_t(_T_DQ, _T_IQ) * (DSA_DH ** -0.5 * LOG2E)).astype(BF16)
    iqt_ref[...] = proj_t(_T_IQ, _T_DV).astype(BF16)
    dvt = proj_t(_T_DV, _T_IW).astype(BF16)
    for c in range(dvt_ref.shape[0]):
        dvt_ref[c, 0:DSA_DH, :] = dvt[:, c * DSA_TK:(c + 1) * DSA_TK]
        dvt_ref[c, DSA_DH:DSA_VROWS, :] = jnp.ones((DSA_VROWS - DSA_DH, DSA_TK), BF16)
    iwt_ref[...] = proj_t(_T_IW, _T_END) * ((IDX_HEADS ** -0.5) * (IDX_DIM ** -0.5))


def _even_proj(x2, gain, w_in, wa2, ba2):
    m = x2.shape[0]
    tm = PROJ_TM
    s = [0, 256, 512, 1024, 1040, 1552, 2064, 2192, 2320, 2832, 2896, 2904]
    gq, gk, gv, ga, gr, dq, dk, dv, iq, ik, iw = [w_in[:, s[i]:s[i + 1]] for i in range(11)]
    zpad = lambda n: jnp.zeros((D_MODEL, n), w_in.dtype)
    w = jnp.concatenate([gq, gk, gv, ga, zpad(LANES - GLA_RANK), gr, dk, ik, zpad(LANES - IDX_DIM)],
                        axis=1).astype(BF16)
    wt = jnp.concatenate([dq, iq, dv, iw, zpad(_T_END - _T_IW - IDX_HEADS)], axis=1).T.astype(BF16)
    wa2p = jnp.concatenate(
        [wa2, jnp.zeros((LANES - GLA_RANK, wa2.shape[1]), wa2.dtype)], axis=0).astype(BF16)
    row = lambda n: pl.BlockSpec((tm, n), lambda i: (i, 0))
    col = lambda n: pl.BlockSpec((n, tm), lambda i: (0, i))
    n_iw = _T_END - _T_IW
    return pl.pallas_call(
        _even_proj_kernel,
        grid=(m // tm,),
        in_specs=[row(D_MODEL), _full((1, D_MODEL)), _full((D_MODEL, _E_END)),
                  _full((_T_END, D_MODEL)), _full((LANES, 256)), _full((1, 256))],
        out_specs=[row(256), row(256), row(512), row(256), row(512), row(DSA_DH), row(IDX_DIM),
                   col(512), col(512),
                   pl.BlockSpec((tm // DSA_TK, DSA_VROWS, DSA_TK), lambda i: (i, 0, 0)),
                   col(n_iw)],
        out_shape=[jax.ShapeDtypeStruct((m, 256), F32), jax.ShapeDtypeStruct((m, 256), F32),
                   jax.ShapeDtypeStruct((m, 512), BF16), jax.ShapeDtypeStruct((m, 256), F32),
                   jax.ShapeDtypeStruct((m, 512), F32), jax.ShapeDtypeStruct((m, DSA_DH), BF16),
                   jax.ShapeDtypeStruct((m, IDX_DIM), BF16),
                   jax.ShapeDtypeStruct((512, m), BF16), jax.ShapeDtypeStruct((512, m), BF16),
                   jax.ShapeDtypeStruct((m // DSA_TK, DSA_VROWS, DSA_TK), BF16),
                   jax.ShapeDtypeStruct((n_iw, m), F32)],
        compiler_params=_cparams("parallel"),
        name="even_proj",
    )(x2, gain.reshape(1, D_MODEL), w, wt, wa2p, ba2.reshape(1, 256))


def _gla_kernel(q_ref, k_ref, v_ref, la_ref, gr_ref, gn_ref, o_ref, st_ref, *, n_chunks):
    c_ = GLA_CHUNK

    @pl.when(pl.program_id(1) == 0)
    def _():
        st_ref[...] = jnp.zeros_like(st_ref)

    ri = lax.broadcasted_iota(jnp.int32, (c_, c_), 0)
    ci = lax.broadcasted_iota(jnp.int32, (c_, c_), 1)
    tri = ri >= ci
    ltri = jnp.where(tri, 1.0, 0.0).astype(BF16)
    gn = gn_ref[...]

    def chunk(c, carry):
        rows = pl.ds(pl.multiple_of(c * c_, c_), c_)
        g = la_ref[rows, :]
        g1 = g.astype(BF16)
        r1 = g - g1.astype(F32)
        g2 = r1.astype(BF16)
        g3 = (r1 - g2.astype(F32)).astype(BF16)
        cum = _dot(ltri, g1) + _dot(ltri, g2) + _dot(ltri, g3)
        last = cum[c_ - 1:c_, :]
        kk = k_ref[rows, :]
        q_t = (q_ref[rows, :] * jnp.exp(cum) * (GLA_DK ** -0.5)).astype(BF16)
        k_t = (kk * jnp.exp(-cum)).astype(BF16)
        k_e = (kk * jnp.exp(last - cum)).astype(BF16)
        dec = jnp.exp(last)
        for h in range(GLA_HEADS):
            ks = slice(h * GLA_DK, (h + 1) * GLA_DK)
            vs = slice(h * GLA_DV, (h + 1) * GLA_DV)
            v_h = v_ref[rows, vs]
            a = jnp.where(tri, _dot_nt(q_t[:, ks], k_t[:, ks]), 0.0).astype(BF16)
            st = st_ref[h]
            o = _dot(a, v_h) + _dot_nt(q_t[:, ks], st.astype(BF16))
            st_ref[h] = dec[:, ks] * st + _dot_tn(v_h, k_e[:, ks])
            y = _rms(o, gn) * _silu(gr_ref[rows, vs])
            o_ref[rows, vs] = y.astype(o_ref.dtype)
        return carry

    lax.fori_loop(0, n_chunks, chunk, 0)


def _gla(gq, gk, gv, la, gr, gnorm, b, t):
    tt = min(GLA_TT, t)
    nt = t // tt
    row = lambda n: pl.BlockSpec((tt, n), lambda i, j: (i * nt + j, 0))
    return pl.pallas_call(
        functools.partial(_gla_kernel, n_chunks=tt // GLA_CHUNK),
        grid=(b, nt),
        in_specs=[row(256), row(256), row(512), row(256), row(512), _full((1, GLA_DV))],
        out_specs=row(512),
        out_shape=jax.ShapeDtypeStruct((b * t, 512), BF16),
        scratch_shapes=[pltpu.VMEM((GLA_HEADS, GLA_DV, GLA_DK), F32)],
        compiler_params=_cparams("parallel", "arbitrary"),
        name="gla",
    )(gq, gk, gv, la, gr, gnorm.reshape(1, GLA_DV))


def _dsa_kernel(iqt_ref, dqt_ref, iwt_ref, ik_ref, dk_ref, dvt_ref, o_ref,
                s_ref, acc_ref, bias_ref, sbuf_ref, *, topk, t_len):
    tq, tk, sk = DSA_TQ, DSA_TK, DSA_SK
    n_sub = tk // sk
    qi = pl.program_id(1)
    q0 = qi * tq
    nkt = (q0 + tq + tk - 1) // tk
    neg_inf = -jnp.inf

    qpos = q0 + lax.broadcasted_iota(jnp.int32, (sk, tq), 1)
    krow = lax.broadcasted_iota(jnp.int32, (sk, tq), 0)
    krow_t = lax.broadcasted_iota(jnp.int32, (tk, tq), 0)

    def colsum(c):
        r = c.shape[0] // (4 * SUBLANES)
        part = jnp.sum(c.reshape(r, 4, SUBLANES, tq), axis=0)
        return jnp.sum(part, axis=0)

    def score_tile(kt, carry, diagonal):
        rmax, rmin = carry
        for j in range(n_sub):
            k0 = pl.multiple_of(kt * tk, tk) + j * sk
            ik_t = ik_ref[pl.ds(k0, sk), :]
            sc = jnp.zeros((sk, tq), F32)
            for h in range(IDX_HEADS):
                lg = _dot(ik_t, iqt_ref[h * IDX_DIM:(h + 1) * IDX_DIM, :])
                sc = sc + iwt_ref[h:h + 1, :] * jnp.maximum(lg, 0.0)
            if diagonal:
                allowed = (k0 + krow) <= qpos
                s_ref[kt, j * sk:(j + 1) * sk, :] = jnp.where(allowed, sc, neg_inf)
                hi_part, lo_part = jnp.where(allowed, sc, neg_inf), jnp.where(allowed, sc, jnp.inf)
            else:
                s_ref[kt, j * sk:(j + 1) * sk, :] = sc
                hi_part, lo_part = sc, sc
            rmax = jnp.maximum(rmax, jnp.max(hi_part.reshape(sk // SUBLANES, SUBLANES, tq), axis=0))
            rmin = jnp.minimum(rmin, jnp.min(lo_part.reshape(sk // SUBLANES, SUBLANES, tq), axis=0))
        return rmax, rmin

    def score_pair(i, carry):
        carry = score_tile(2 * i, carry, False)
        return score_tile(jnp.minimum(2 * i + 1, nkt - 2), carry, False)

    carry = lax.fori_loop(0, nkt // 2, score_pair,
                          (jnp.full((SUBLANES, tq), neg_inf, F32), jnp.full((SUBLANES, tq), jnp.inf, F32)))
    rmax, rmin = score_tile(nkt - 1, carry, True)
    rmax = jnp.max(rmax, axis=0, keepdims=True)
    rmin = jnp.min(rmin, axis=0, keepdims=True)

    def count(ind_fn):
        def body(kt, acc):
            return acc + colsum(ind_fn(s_ref[kt], kt))
        acc = lax.fori_loop(0, nkt, body, jnp.zeros((SUBLANES, tq), F32))
        return jnp.sum(acc, axis=0, keepdims=True)

    kf = jnp.float32(topk)
    n_allowed = (qpos[0:1, :] + 1).astype(F32)

    def bisect(_, carry):
        lo, hi, c_lo = carry
        mid = lo + (hi - lo) * 0.5
        c = count(lambda s, kt: jnp.where(s >= mid, 1.0, 0.0))
        ge = c >= kf
        return jnp.where(ge, mid, lo), jnp.where(ge, hi, mid), jnp.where(ge, c, c_lo)

    lo, _, c_lo = lax.fori_loop(0, BISECT_ITERS, bisect, (rmin, rmax, n_allowed))

    def colmin(c):
        return jnp.min(c.reshape(tk // SUBLANES, SUBLANES, tq), axis=0)

    def snap(kt, acc):
        s = s_ref[kt]
        return jnp.minimum(acc, colmin(jnp.where(s >= lo, s, jnp.inf)))

    v0 = jnp.min(lax.fori_loop(0, nkt, snap, jnp.full((SUBLANES, tq), jnp.inf, F32)),
                 axis=0, keepdims=True)

    def above(v):
        def body(kt, carry):
            cnt, mn = carry
            s = s_ref[kt]
            gt = s > v
            return (cnt + colsum(jnp.where(gt, 1.0, 0.0)),
                    jnp.minimum(mn, colmin(jnp.where(gt, s, jnp.inf))))
        cnt, mn = lax.fori_loop(0, nkt, body, (jnp.zeros((SUBLANES, tq), F32),
                                                jnp.full((SUBLANES, tq), jnp.inf, F32)))
        return jnp.sum(cnt, axis=0, keepdims=True), jnp.min(mn, axis=0, keepdims=True)

    def step_up(carry):
        v, f_v, _, _ = carry
        n_gt, v_next = above(v)
        fin = n_gt < kf
        pending = jnp.max(jnp.where(fin, 0.0, 1.0))
        return jnp.where(fin, v, v_next), jnp.where(fin, f_v, n_gt), n_gt, pending

    thr, f_thr, n_gt, _ = lax.while_loop(
        lambda carry: carry[3] > 0.0, step_up, (v0, c_lo, jnp.zeros((1, tq), F32), jnp.float32(1.0)))
    n_eq = f_thr - n_gt
    need = kf - n_gt

    @pl.when(jnp.max(n_eq - need) > 0.0)
    def _():
        def ibisect(_, carry):
            ilo, ihi = carry
            imid = jnp.right_shift(ilo + ihi, 1)
            c = count(lambda s, kt: jnp.where(
                s == thr, jnp.where((kt * tk + krow_t) <= imid, 1.0, 0.0), 0.0))
            ge = c >= need
            return jnp.where(ge, ilo, imid), jnp.where(ge, imid, ihi)
        ilo0 = jnp.full((1, tq), -1, jnp.int32)
        ihi0 = jnp.full((1, tq), t_len - 1, jnp.int32)
        n_it = max(1, int(t_len - 1).bit_length() + 1)
        jlim = lax.fori_loop(0, n_it, ibisect, (ilo0, ihi0))[1]

        def drop(kt, carry):
            s = s_ref[kt]
            s_ref[kt] = jnp.where(s == thr, jnp.where((kt * tk + krow_t) <= jlim, s, neg_inf), s)
            return carry
        lax.fori_loop(0, nkt, drop, 0)

    acc_ref[...] = jnp.zeros(acc_ref.shape, F32)

    def logits(t, slot):
        tc = jnp.minimum(t, nkt - 1)
        thr_eff = jnp.where(t < nkt, thr, jnp.inf)
        bias_ref[slot] = jnp.where(s_ref[tc] >= thr_eff, 0.0, neg_inf)
        k_t = dk_ref[pl.ds(pl.multiple_of(tc * tk, tk), tk), :]
        for h in range(DSA_HEADS):
            sbuf_ref[slot, h] = _dot(k_t, dqt_ref[h * DSA_DH:(h + 1) * DSA_DH, :]) + bias_ref[slot]

    def softmax_pv(t, slot, ms):
        ms = list(ms)
        vt_t = dvt_ref[jnp.minimum(t, nkt - 1)]
        for h in range(DSA_HEADS):
            m_new = jnp.maximum(ms[h], jnp.max(sbuf_ref[slot, h], axis=0, keepdims=True))
            m_safe = jnp.where(m_new == neg_inf, 0.0, m_new)
            alpha = jnp.exp2(ms[h] - m_safe)
            p = jnp.exp2(sbuf_ref[slot, h] - m_safe).astype(BF16)
            acc_ref[h] = alpha * acc_ref[h] + _dot(vt_t, p)
            ms[h] = m_new
        return tuple(ms)

    def attend(i, ms):
        logits(2 * i + 1, 1)
        ms = softmax_pv(2 * i, 0, ms)
        logits(2 * i + 2, 0)
        return softmax_pv(2 * i + 1, 1, ms)

    logits(0, 0)
    m0 = tuple(jnp.full((1, tq), neg_inf, F32) for _ in range(DSA_HEADS))
    lax.fori_loop(0, (nkt + 1) // 2, attend, m0)
    for h in range(DSA_HEADS):
        num = acc_ref[h, 0:DSA_DH, :]
        den = acc_ref[h, DSA_DH:DSA_DH + 1, :]
        o_ref[:, h * DSA_DH:(h + 1) * DSA_DH] = (num / den).T.astype(o_ref.dtype)


def _dsa(iqt, dqt, iwt, ik, dk, dvt, b, t):
    tq, tk, sk = DSA_TQ, DSA_TK, DSA_SK
    assert tq == tk and t % tq == 0
    nq = t // tq
    topk = min(TOPK_MAX, t // 4)
    qcol = lambda n: pl.BlockSpec((n, tq), lambda i, j: (0, i * nq + j))
    krow = lambda n: pl.BlockSpec((t, n), lambda i, j: (i, 0))
    return pl.pallas_call(
        functools.partial(_dsa_kernel, topk=topk, t_len=t),
        grid=(b, nq),
        in_specs=[qcol(512), qcol(512), qcol(iwt.shape[0]), krow(IDX_DIM), krow(DSA_DH),
                  pl.BlockSpec((t // tk, DSA_VROWS, tk), lambda i, j: (i, 0, 0))],
        out_specs=pl.BlockSpec((tq, 512), lambda i, j: (i * nq + j, 0)),
        out_shape=jax.ShapeDtypeStruct((b * t, 512), BF16),
        scratch_shapes=[
            pltpu.VMEM((t // tk, tk, tq), F32),
            pltpu.VMEM((DSA_HEADS, DSA_VROWS, tq), F32),
            pltpu.VMEM((2, tk, tq), F32),
            pltpu.VMEM((2, DSA_HEADS, tk, tq), F32),
        ],
        compiler_params=_cparams("parallel", "arbitrary"),
        name="dsa",
    )(iqt, dqt, iwt, ik, dk, dvt)


def _out_proj_kernel(*refs, n_in):
    res_ref = refs[0]
    a_refs = refs[1:1 + n_in]
    w_refs = refs[1 + n_in:1 + 2 * n_in]
    o_ref = refs[1 + 2 * n_in]
    acc = res_ref[...]
    for a, w in zip(a_refs, w_refs):
        acc = acc + _dot(a[...], w[...])
    o_ref[...] = acc


def _out_proj(res, acts, ws):
    m = res.shape[0]
    tm = PROJ_TM
    row = lambda n: pl.BlockSpec((tm, n), lambda i: (i, 0))
    return pl.pallas_call(
        functools.partial(_out_proj_kernel, n_in=len(acts)),
        grid=(m // tm,),
        in_specs=[row(D_MODEL)] + [row(a.shape[1]) for a in acts] + [_full(w.shape) for w in ws],
        out_specs=row(D_MODEL),
        out_shape=jax.ShapeDtypeStruct((m, D_MODEL), F32),
        compiler_params=_cparams("parallel"),
        name="out_proj",
    )(res, *acts, *ws)


def _ffn_kernel(h_ref, g_ref, wg_ref, wu_ref, wd_ref, fg_ref, o_ref, *, final_norm):
    x = h_ref[...]
    xn = _rms(x, g_ref[...]).astype(BF16)
    acc = x
    for a, b in FFN_CHUNKS:
        t = _silu(_dot(xn, wg_ref[:, a:b])) * _dot(xn, wu_ref[:, a:b])
        acc = acc + _dot(t.astype(BF16), wd_ref[a:b, :])
    o_ref[...] = _rms(acc, fg_ref[...]) if final_norm else acc


def _ffn(h, gain, wg, wu, wd, final_gain=None):
    m = h.shape[0]
    tm = FFN_TM
    fg = jnp.ones((1, D_MODEL), F32) if final_gain is None else final_gain.reshape(1, D_MODEL)
    row = pl.BlockSpec((tm, D_MODEL), lambda i: (i, 0))
    return pl.pallas_call(
        functools.partial(_ffn_kernel, final_norm=final_gain is not None),
        grid=(m // tm,),
        in_specs=[row, _full((1, D_MODEL)), _resident((D_MODEL, D_FF)), _resident((D_MODEL, D_FF)),
                  _resident((D_FF, D_MODEL)), _full((1, D_MODEL))],
        out_specs=row,
        out_shape=jax.ShapeDtypeStruct((m, D_MODEL), F32),
        compiler_params=_cparams("parallel"),
        name="ffn",
    )(h, gain.reshape(1, D_MODEL), wg.astype(BF16), wu.astype(BF16), wd.astype(BF16), fg)


def _odd_proj_kernel(x_ref, g_ref, w_ref, cos_ref, sin_ref, q_ref, k_ref, v_ref, gate_ref):
    xn = _rms(x_ref[...], g_ref[...]).astype(BF16)
    cos = cos_ref[...]
    sin = sin_ref[...]
    half = RET_DK // 2

    def rot(z, out_ref, scale):
        for h in range(RET_HEADS):
            x1 = z[:, h * RET_DK:h * RET_DK + half]
            x2 = z[:, h * RET_DK + half:(h + 1) * RET_DK]
            out_ref[:, h * RET_DK:h * RET_DK + half] = ((x1 * cos - x2 * sin) * scale).astype(BF16)
            out_ref[:, h * RET_DK + half:(h + 1) * RET_DK] = ((x2 * cos + x1 * sin) * scale).astype(BF16)

    nq = RET_HEADS * RET_DK
    nv = RET_HEADS * RET_DV
    rot(_dot(xn, w_ref[:, 0:nq]), q_ref, 1.0)
    rot(_dot(xn, w_ref[:, nq:2 * nq]), k_ref, RET_DK ** -0.5)
    v_ref[...] = _dot(xn, w_ref[:, 2 * nq:2 * nq + nv]).astype(BF16)
    gate_ref[...] = _dot(xn, w_ref[:, 2 * nq + nv:2 * nq + 2 * nv])


def _odd_proj(h, gain, w_in, t):
    m = h.shape[0]
    tm = PROJ_TM
    nt = t // tm
    half = RET_DK // 2
    inv = ROPE_BASE ** (-jnp.arange(half, dtype=F32) / half)
    ang = jnp.arange(t, dtype=jnp.int32).astype(F32)[:, None] * inv[None, :]
    row = lambda n: pl.BlockSpec((tm, n), lambda i: (i, 0))
    pos = pl.BlockSpec((tm, half), lambda i: (i % nt, 0))
    nq = RET_HEADS * RET_DK
    nv = RET_HEADS * RET_DV
    return pl.pallas_call(
        _odd_proj_kernel,
        grid=(m // tm,),
        in_specs=[row(D_MODEL), _full((1, D_MODEL)), _full(w_in.shape), pos, pos],
        out_specs=[row(nq), row(nq), row(nv), row(nv)],
        out_shape=[jax.ShapeDtypeStruct((m, nq), BF16), jax.ShapeDtypeStruct((m, nq), BF16),
                   jax.ShapeDtypeStruct((m, nv), BF16), jax.ShapeDtypeStruct((m, nv), F32)],
        compiler_params=_cparams("parallel"),
        name="odd_proj",
    )(h, gain.reshape(1, D_MODEL), w_in.astype(BF16), jnp.cos(ang), jnp.sin(ang))


def _ret_kernel(dec_ref, q_ref, k_ref, v_ref, gate_ref, dm_ref, xi_ref, ze_ref, rn_ref,
                o_ref, st_ref, *, n_chunks):
    c_ = RET_CHUNK

    @pl.when(pl.program_id(1) == 0)
    def _():
        st_ref[...] = jnp.zeros_like(st_ref)

    rn = rn_ref[...]
    for c in range(n_chunks):
        rows = slice(c * c_, (c + 1) * c_)
        for h in range(RET_HEADS):
            ks = slice(h * RET_DK, (h + 1) * RET_DK)
            vs = slice(h * RET_DV, (h + 1) * RET_DV)
            q = q_ref[rows, ks]
            k = k_ref[rows, ks]
            v = v_ref[rows, vs]
            st = st_ref[h]
            inner = (_dot_nt(q, k) * dm_ref[h]).astype(BF16)
            o = _dot(inner, v) + xi_ref[h] * _dot(q, st.astype(BF16))
            kz = (k.astype(F32) * ze_ref[h]).astype(BF16)
            st_ref[h] = dec_ref[h] * st + _dot_tn(kz, v)
            y = _rms(o, rn) * _silu(gate_ref[rows, vs])
            o_ref[rows, vs] = y.astype(o_ref.dtype)


def _retention(q, k, v, gate, rnorm, b, t):
    c_ = min(RET_CHUNK, t)
    tt = min(RET_TT, t)
    nt = t // tt
    log_g = jnp.log1p(-jnp.exp2(-5.0 - jnp.arange(RET_HEADS, dtype=F32)))
    idx = jnp.arange(c_, dtype=F32)
    rel = idx[:, None] - idx[None, :]
    dmat = jnp.where(rel >= 0, jnp.exp(log_g[:, None, None] * jnp.maximum(rel, 0.0)), 0.0)
    xi = jnp.exp(log_g[:, None] * (idx[None, :] + 1.0))[:, :, None]
    zeta = jnp.exp(log_g[:, None] * (c_ - 1.0 - idx[None, :]))[:, :, None]
    decay_c = jnp.exp(log_g * c_)
    nq = RET_HEADS * RET_DK
    nv = RET_HEADS * RET_DV
    row = lambda n: pl.BlockSpec((tt, n), lambda i, j: (i * nt + j, 0))
    return pl.pallas_call(
        functools.partial(_ret_kernel, n_chunks=tt // c_),
        grid=(b, nt),
        in_specs=[pl.BlockSpec(memory_space=pltpu.SMEM),
                  row(nq), row(nq), row(nv), row(nv),
                  _full((RET_HEADS, c_, c_)), _full((RET_HEADS, c_, 1)), _full((RET_HEADS, c_, 1)),
                  _full((1, RET_DV))],
        out_specs=row(nv),
        out_shape=jax.ShapeDtypeStruct((b * t, nv), BF16),
        scratch_shapes=[pltpu.VMEM((RET_HEADS, RET_DK, RET_DV), F32)],
        compiler_params=_cparams("parallel", "arbitrary"),
        name="retention",
    )(decay_c, q, k, v, gate, dmat, xi, zeta, rnorm.reshape(1, RET_DV))


def kernel(x, even_attn_norm, even_w_in, even_gla_wa2, even_gla_ba2, even_gla_norm, even_w_out,
           odd_attn_norm, odd_w_in, odd_ret_norm, odd_w_out,
           ffn_norm, ffn_w_gate, ffn_w_up, ffn_w_down, final_norm):
    b, t, d = x.shape
    h = x.reshape(b * t, d)

    gq, gk, gv, la, gr, dk, ik, dqt, iqt, dvt, iwt = _even_proj(
        h, even_attn_norm[0], even_w_in[0], even_gla_wa2[0], even_gla_ba2[0])
    o_gla = _gla(gq, gk, gv, la, gr, even_gla_norm[0], b, t)
    o_dsa = _dsa(iqt, dqt, iwt, ik, dk, dvt, b, t)
    w_out = even_w_out[0].astype(BF16)
    n_gla = GLA_HEADS * GLA_DV
    h = _out_proj(h, [o_gla, o_dsa], [w_out[:n_gla], w_out[n_gla:]])
    h = _ffn(h, ffn_norm[0], ffn_w_gate[0], ffn_w_up[0], ffn_w_down[0])

    q, k, v, gate = _odd_proj(h, odd_attn_norm[0], odd_w_in[0], t)
    o_ret = _retention(q, k, v, gate, odd_ret_norm[0], b, t)
    h = _out_proj(h, [o_ret], [odd_w_out[0].astype(BF16)])
    h = _ffn(h, ffn_norm[1], ffn_w_gate[1], ffn_w_up[1], ffn_w_down[1], final_gain=final_norm)
    return h.reshape(b, t, d)
```

```python
import functools

import jax
import jax.numpy as jnp
from jax import lax
from jax.experimental import pallas as pl
from jax.experimental.pallas import tpu as pltpu

F32 = jnp.float32
BF16 = jnp.bfloat16

D_MODEL = 1024
EPS = 1e-6
GLA_HEADS = 4
GLA_DV = 128
GLA_DK = 64
GLA_RANK = 16
GLA_GATE_NORM = 16.0
GLA_CHUNK = 64
DSA_HEADS = 4
DSA_DH = 128
IDX_HEADS = 8
IDX_DIM = 64
TOPK_MAX = 256
RET_HEADS = 4
RET_DK = 256
RET_DV = 512
ROPE_BASE = 10000.0
D_FF = 2816

LANES = 128
SUBLANES = 8
VMEM_LIMIT = 56 * 1024 * 1024

PROJ_TM = 512
FFN_TM = 512
FFN_CHUNKS = ((0, 1024), (1024, 2048), (2048, 2816))
GLA_TT = 512
RET_CHUNK = 256
RET_TT = 512
DSA_TQ = 256
DSA_TK = 256
DSA_SK = 128
DSA_VROWS = DSA_DH + 16
LOG2E = 1.4426950408889634
BISECT_ITERS = 16


def _cparams(*sem):
    return pltpu.CompilerParams(dimension_semantics=sem, vmem_limit_bytes=VMEM_LIMIT)


def _dot(a, b):
    return jnp.dot(a, b, preferred_element_type=F32)


def _dot_nt(a, b):
    return lax.dot_general(a, b, (((1,), (1,)), ((), ())), preferred_element_type=F32)


def _dot_tn(a, b):
    return lax.dot_general(a, b, (((0,), (0,)), ((), ())), preferred_element_type=F32)


def _rms(x, g):
    return x * lax.rsqrt(jnp.mean(x * x, axis=-1, keepdims=True) + EPS) * g


def _silu(x):
    return x * jax.nn.sigmoid(x)


def _full(shape):
    return pl.BlockSpec(shape, lambda *_: (0,) * len(shape))


def _resident(shape):
    return pl.BlockSpec(shape, lambda *_: (0,) * len(shape), pipeline_mode=pl.Buffered(1))


_E_GQ, _E_GK, _E_GV, _E_GA, _E_GR, _E_DK, _E_IK, _E_END = (0, 256, 512, 1024, 1152, 1664, 1792, 1920)
_T_DQ, _T_IQ, _T_DV, _T_IW, _T_END = (0, 512, 1024, 1152, 1168)


def _even_proj_kernel(x_ref, g_ref, w_ref, wt_ref, wa2_ref, ba2_ref,
                      gq_ref, gk_ref, gv_ref, la_ref, gr_ref, dk_ref, ik_ref,
                      dqt_ref, iqt_ref, dvt_ref, iwt_ref):
    xn = _rms(x_ref[...], g_ref[...]).astype(BF16)

    def proj(a, b):
        return _dot(xn, w_ref[:, a:b])

    def proj_t(a, b):
        return _dot_nt(wt_ref[a:b, :], xn)

    gq_ref[...] = proj(_E_GQ, _E_GK)
    gk_ref[...] = proj(_E_GK, _E_GV)
    gv_ref[...] = proj(_E_GV, _E_GA).astype(BF16)
    ga = proj(_E_GA, _E_GR).astype(BF16)
    z = _dot(ga, wa2_ref[...]) + ba2_ref[...]
    la_ref[...] = jax.nn.log_sigmoid(z) * (1.0 / GLA_GATE_NORM)
    gr_ref[...] = proj(_E_GR, _E_DK)
    dk_ref[...] = proj(_E_DK, _E_IK).astype(BF16)
    ik_ref[...] = proj(_E_IK, _E_IK + IDX_DIM).astype(BF16)
    dqt_ref[...] = (proj_t(_T_DQ, _T_IQ) * (DSA_DH ** -0.5 * LOG2E)).astype(BF16)
    iqt_ref[...] = proj_t(_T_IQ, _T_DV).astype(BF16)
    dvt = proj_t(_T_DV, _T_IW).astype(BF16)
    for c in range(dvt_ref.shape[0]):
        dvt_ref[c, 0:DSA_DH, :] = dvt[:, c * DSA_TK:(c + 1) * DSA_TK]
        dvt_ref[c, DSA_DH:DSA_VROWS, :] = jnp.ones((DSA_VROWS - DSA_DH, DSA_TK), BF16)
    iwt_ref[...] = proj_t(_T_IW, _T_END) * ((IDX_HEADS ** -0.5) * (IDX_DIM ** -0.5))


def _even_proj(x2, gain, w_in, wa2, ba2):
    m = x2.shape[0]
    tm = PROJ_TM
    s = [0, 256, 512, 1024, 1040, 1552, 2064, 2192, 2320, 2832, 2896, 2904]
    gq, gk, gv, ga, gr, dq, dk, dv, iq, ik, iw = [w_in[:, s[i]:s[i + 1]] for i in range(11)]
    zpad = lambda n: jnp.zeros((D_MODEL, n), w_in.dtype)
    w = jnp.concatenate([gq, gk, gv, ga, zpad(LANES - GLA_RANK), gr, dk, ik, zpad(LANES - IDX_DIM)],
                        axis=1).astype(BF16)
    wt = jnp.concatenate([dq, iq, dv, iw, zpad(_T_END - _T_IW - IDX_HEADS)], axis=1).T.astype(BF16)
    wa2p = jnp.concatenate(
        [wa2, jnp.zeros((LANES - GLA_RANK, wa2.shape[1]), wa2.dtype)], axis=0).astype(BF16)
    row = lambda n: pl.BlockSpec((tm, n), lambda i: (i, 0))
    col = lambda n: pl.BlockSpec((n, tm), lambda i: (0, i))
    n_iw = _T_END - _T_IW
    return pl.pallas_call(
        _even_proj_kernel,
        grid=(m // tm,),
        in_specs=[row(D_MODEL), _full((1, D_MODEL)), _full((D_MODEL, _E_END)),
                  _full((_T_END, D_MODEL)), _full((LANES, 256)), _full((1, 256))],
        out_specs=[row(256), row(256), row(512), row(256), row(512), row(DSA_DH), row(IDX_DIM),
                   col(512), col(512),
                   pl.BlockSpec((tm // DSA_TK, DSA_VROWS, DSA_TK), lambda i: (i, 0, 0)),
                   col(n_iw)],
        out_shape=[jax.ShapeDtypeStruct((m, 256), F32), jax.ShapeDtypeStruct((m, 256), F32),
                   jax.ShapeDtypeStruct((m, 512), BF16), jax.ShapeDtypeStruct((m, 256), F32),
                   jax.ShapeDtypeStruct((m, 512), F32), jax.ShapeDtypeStruct((m, DSA_DH), BF16),
                   jax.ShapeDtypeStruct((m, IDX_DIM), BF16),
                   jax.ShapeDtypeStruct((512, m), BF16), jax.ShapeDtypeStruct((512, m), BF16),
                   jax.ShapeDtypeStruct((m // DSA_TK, DSA_VROWS, DSA_TK), BF16),
                   jax.ShapeDtypeStruct((n_iw, m), F32)],
        compiler_params=_cparams("parallel"),
        name="even_proj",
    )(x2, gain.reshape(1, D_MODEL), w, wt, wa2p, ba2.reshape(1, 256))


def _gla_kernel(q_ref, k_ref, v_ref, la_ref, gr_ref, gn_ref, o_ref, st_ref, *, n_chunks):
    c_ = GLA_CHUNK

    @pl.when(pl.program_id(1) == 0)
    def _():
        st_ref[...] = jnp.zeros_like(st_ref)

    ri = lax.broadcasted_iota(jnp.int32, (c_, c_), 0)
    ci = lax.broadcasted_iota(jnp.int32, (c_, c_), 1)
    tri = ri >= ci
    ltri = jnp.where(tri, 1.0, 0.0).astype(BF16)
    gn = gn_ref[...]

    chunks = range(n_chunks)
    heads = range(GLA_HEADS)
    rows = [slice(c * c_, (c + 1) * c_) for c in chunks]
    vs = [slice(h * GLA_DV, (h + 1) * GLA_DV) for h in heads]

    def cumsum(g):
        g1 = g.astype(BF16)
        r1 = g - g1.astype(F32)
        g2 = r1.astype(BF16)
        g3 = (r1 - g2.astype(F32)).astype(BF16)
        return _dot(ltri, g1) + _dot(ltri, g2) + _dot(ltri, g3)

    low_half = lax.broadcasted_iota(jnp.int32, (c_, LANES), 1) < GLA_DK
    pair = [slice((h // 2) * LANES, (h // 2 + 1) * LANES) for h in heads]

    def half(z, h):
        zp = z[:, pair[h]]
        return (jnp.where(low_half, zp, 0.0) if h % 2 == 0 else jnp.where(low_half, 0.0, zp)).astype(BF16)

    cum = [cumsum(la_ref[rows[c], :]) for c in chunks]
    last = [cum[c][c_ - 1:c_, :] for c in chunks]
    q_t = [(q_ref[rows[c], :] * jnp.exp(cum[c]) * (GLA_DK ** -0.5)).astype(BF16) for c in chunks]
    k_t = [k_ref[rows[c], :] * jnp.exp(-cum[c]) for c in chunks]
    k_e = [k_ref[rows[c], :] * jnp.exp(last[c] - cum[c]) for c in chunks]
    dec = [jnp.exp(last[c]) for c in chunks]
    a = [[jnp.where(tri, _dot_nt(q_t[c][:, pair[h]], half(k_t[c], h)), 0.0).astype(BF16)
          for h in heads] for c in chunks]
    o_intra = [[_dot(a[c][h], v_ref[rows[c], vs[h]]) for h in heads] for c in chunks]
    kv = [[_dot_tn(v_ref[rows[c], vs[h]], half(k_e[c], h)) for h in heads] for c in chunks]
    for h in heads:
        st = st_ref[h]
        for c in chunks:
            o = o_intra[c][h] + _dot_nt(q_t[c][:, pair[h]], st.astype(BF16))
            st = dec[c][:, pair[h]] * st + kv[c][h]
            y = _rms(o, gn) * _silu(gr_ref[rows[c], vs[h]])
            o_ref[rows[c], vs[h]] = y.astype(o_ref.dtype)
        st_ref[h] = st


def _gla(gq, gk, gv, la, gr, gnorm, b, t):
    tt = min(GLA_TT, t)
    nt = t // tt
    row = lambda n: pl.BlockSpec((tt, n), lambda i, j: (i * nt + j, 0))
    return pl.pallas_call(
        functools.partial(_gla_kernel, n_chunks=tt // GLA_CHUNK),
        grid=(b, nt),
        in_specs=[row(256), row(256), row(512), row(256), row(512), _full((1, GLA_DV))],
        out_specs=row(512),
        out_shape=jax.ShapeDtypeStruct((b * t, 512), BF16),
        scratch_shapes=[pltpu.VMEM((GLA_HEADS, GLA_DV, LANES), F32)],
        compiler_params=_cparams("parallel", "arbitrary"),
        name="gla",
    )(gq, gk, gv, la, gr, gnorm.reshape(1, GLA_DV))


def _dsa_kernel(iqt_ref, dqt_ref, iwt_ref, ik_ref, dk_ref, dvt_ref, o_ref,
                s_ref, acc_ref, bias_ref, sbuf_ref, *, topk, t_len):
    tq, tk, sk = DSA_TQ, DSA_TK, DSA_SK
    n_sub = tk // sk
    qi = pl.program_id(1)
    q0 = qi * tq
    nkt = (q0 + tq + tk - 1) // tk
    neg_inf = -jnp.inf

    qpos = q0 + lax.broadcasted_iota(jnp.int32, (sk, tq), 1)
    krow = lax.broadcasted_iota(jnp.int32, (sk, tq), 0)
    krow_t = lax.broadcasted_iota(jnp.int32, (tk, tq), 0)

    def colsum(c):
        r = c.shape[0] // (4 * SUBLANES)
        part = jnp.sum(c.reshape(r, 4, SUBLANES, tq), axis=0)
        return jnp.sum(part, axis=0)

    def score_tile(kt, carry, diagonal):
        rmax, rmin = carry
        for j in range(n_sub):
            k0 = pl.multiple_of(kt * tk, tk) + j * sk
            ik_t = ik_ref[pl.ds(k0, sk), :]
            sc = jnp.zeros((sk, tq), F32)
            for h in range(IDX_HEADS):
                lg = _dot(ik_t, iqt_ref[h * IDX_DIM:(h + 1) * IDX_DIM, :])
                sc = sc + iwt_ref[h:h + 1, :] * jnp.maximum(lg, 0.0)
            if diagonal:
                allowed = (k0 + krow) <= qpos
                s_ref[kt, j * sk:(j + 1) * sk, :] = jnp.where(allowed, sc, neg_inf)
                hi_part, lo_part = jnp.where(allowed, sc, neg_inf), jnp.where(allowed, sc, jnp.inf)
            else:
                s_ref[kt, j * sk:(j + 1) * sk, :] = sc
                hi_part, lo_part = sc, sc
            rmax = jnp.maximum(rmax, jnp.max(hi_part.reshape(sk // SUBLANES, SUBLANES, tq), axis=0))
            rmin = jnp.minimum(rmin, jnp.min(lo_part.reshape(sk // SUBLANES, SUBLANES, tq), axis=0))
        return rmax, rmin

    def score_pair(i, carry):
        carry = score_tile(2 * i, carry, False)
        return score_tile(jnp.minimum(2 * i + 1, nkt - 2), carry, False)

    carry = lax.fori_loop(0, nkt // 2, score_pair,
                          (jnp.full((SUBLANES, tq), neg_inf, F32), jnp.full((SUBLANES, tq), jnp.inf, F32)))
    rmax, rmin = score_tile(nkt - 1, carry, True)
    rmax = jnp.max(rmax, axis=0, keepdims=True)
    rmin = jnp.min(rmin, axis=0, keepdims=True)

    def count(ind_fn):
        def body(kt, acc):
            return acc + colsum(ind_fn(s_ref[kt], kt))
        acc = lax.fori_loop(0, nkt, body, jnp.zeros((SUBLANES, tq), F32))
        return jnp.sum(acc, axis=0, keepdims=True)

    kf = jnp.float32(topk)
    n_allowed = (qpos[0:1, :] + 1).astype(F32)

    def bisect(_, carry):
        lo, hi, c_lo = carry
        mid = lo + (hi - lo) * 0.5
        c = count(lambda s, kt: jnp.where(s >= mid, 1.0, 0.0))
        ge = c >= kf
        return jnp.where(ge, mid, lo), jnp.where(ge, hi, mid), jnp.where(ge, c, c_lo)

    lo, _, c_lo = lax.fori_loop(0, BISECT_ITERS, bisect, (rmin, rmax, n_allowed))

    def colmin(c):
        return jnp.min(c.reshape(tk // SUBLANES, SUBLANES, tq), axis=0)

    def snap(kt, acc):
        s = s_ref[kt]
        return jnp.minimum(acc, colmin(jnp.where(s >= lo, s, jnp.inf)))

    v0 = jnp.min(lax.fori_loop(0, nkt, snap, jnp.full((SUBLANES, tq), jnp.inf, F32)),
                 axis=0, keepdims=True)

    def above(v):
        def body(kt, carry):
            cnt, mn = carry
            s = s_ref[kt]
            gt = s > v
            return (cnt + colsum(jnp.where(gt, 1.0, 0.0)),
                    jnp.minimum(mn, colmin(jnp.where(gt, s, jnp.inf))))
        cnt, mn = lax.fori_loop(0, nkt, body, (jnp.zeros((SUBLANES, tq), F32),
                                                jnp.full((SUBLANES, tq), jnp.inf, F32)))
        return jnp.sum(cnt, axis=0, keepdims=True), jnp.min(mn, axis=0, keepdims=True)

    def step_up(carry):
        v, f_v, _, _ = carry
        n_gt, v_next = above(v)
        fin = n_gt < kf
        pending = jnp.max(jnp.where(fin, 0.0, 1.0))
        return jnp.where(fin, v, v_next), jnp.where(fin, f_v, n_gt), n_gt, pending

    thr, f_thr, n_gt, _ = lax.while_loop(
        lambda carry: carry[3] > 0.0, step_up, (v0, c_lo, jnp.zeros((1, tq), F32), jnp.float32(1.0)))
    n_eq = f_thr - n_gt
    need = kf - n_gt

    @pl.when(jnp.max(n_eq - need) > 0.0)
    def _():
        def ibisect(_, carry):
            ilo, ihi = carry
            imid = jnp.right_shift(ilo + ihi, 1)
            c = count(lambda s, kt: jnp.where(
                s == thr, jnp.where((kt * tk + krow_t) <= imid, 1.0, 0.0), 0.0))
            ge = c >= need
            return jnp.where(ge, ilo, imid), jnp.where(ge, imid, ihi)
        ilo0 = jnp.full((1, tq), -1, jnp.int32)
        ihi0 = jnp.full((1, tq), t_len - 1, jnp.int32)
        n_it = max(1, int(t_len - 1).bit_length() + 1)
        jlim = lax.fori_loop(0, n_it, ibisect, (ilo0, ihi0))[1]

        def drop(kt, carry):
            s = s_ref[kt]
            s_ref[kt] = jnp.where(s == thr, jnp.where((kt * tk + krow_t) <= jlim, s, neg_inf), s)
            return carry
        lax.fori_loop(0, nkt, drop, 0)

    acc_ref[...] = jnp.zeros(acc_ref.shape, F32)

    def logits(t, slot):
        tc = jnp.minimum(t, nkt - 1)
        thr_eff = jnp.where(t < nkt, thr, jnp.inf)
        bias_ref[slot] = jnp.where(s_ref[tc] >= thr_eff, 0.0, neg_inf)
        k_t = dk_ref[pl.ds(pl.multiple_of(tc * tk, tk), tk), :]
        for h in range(DSA_HEADS):
            sbuf_ref[slot, h] = _dot(k_t, dqt_ref[h * DSA_DH:(h + 1) * DSA_DH, :]) + bias_ref[slot]

    def softmax_pv(t, slot, ms):
        ms = list(ms)
        vt_t = dvt_ref[jnp.minimum(t, nkt - 1)]
        for h in range(DSA_HEADS):
            m_new = jnp.maximum(ms[h], jnp.max(sbuf_ref[slot, h], axis=0, keepdims=True))
            m_safe = jnp.where(m_new == neg_inf, 0.0, m_new)
            alpha = jnp.exp2(ms[h] - m_safe)
            p = jnp.exp2(sbuf_ref[slot, h] - m_safe).astype(BF16)
            acc_ref[h] = alpha * acc_ref[h] + _dot(vt_t, p)
            ms[h] = m_new
        return tuple(ms)

    def attend(i, ms):
        logits(2 * i + 1, 1)
        ms = softmax_pv(2 * i, 0, ms)
        logits(2 * i + 2, 0)
        return softmax_pv(2 * i + 1, 1, ms)

    logits(0, 0)
    m0 = tuple(jnp.full((1, tq), neg_inf, F32) for _ in range(DSA_HEADS))
    lax.fori_loop(0, (nkt + 1) // 2, attend, m0)
    for h in range(DSA_HEADS):
        num = acc_ref[h, 0:DSA_DH, :]
        den = acc_ref[h, DSA_DH:DSA_DH + 1, :]
        o_ref[:, h * DSA_DH:(h + 1) * DSA_DH] = (num / den).T.astype(o_ref.dtype)


def _dsa(iqt, dqt, iwt, ik, dk, dvt, b, t):
    tq, tk, sk = DSA_TQ, DSA_TK, DSA_SK
    assert tq == tk and t % tq == 0
    nq = t // tq
    topk = min(TOPK_MAX, t // 4)
    qcol = lambda n: pl.BlockSpec((n, tq), lambda i, j: (0, i * nq + j))
    krow = lambda n: pl.BlockSpec((t, n), lambda i, j: (i, 0))
    return pl.pallas_call(
        functools.partial(_dsa_kernel, topk=topk, t_len=t),
        grid=(b, nq),
        in_specs=[qcol(512), qcol(512), qcol(iwt.shape[0]), krow(IDX_DIM), krow(DSA_DH),
                  pl.BlockSpec((t // tk, DSA_VROWS, tk), lambda i, j: (i, 0, 0))],
        out_specs=pl.BlockSpec((tq, 512), lambda i, j: (i * nq + j, 0)),
        out_shape=jax.ShapeDtypeStruct((b * t, 512), BF16),
        scratch_shapes=[
            pltpu.VMEM((t // tk, tk, tq), F32),
            pltpu.VMEM((DSA_HEADS, DSA_VROWS, tq), F32),
            pltpu.VMEM((2, tk, tq), F32),
            pltpu.VMEM((2, DSA_HEADS, tk, tq), F32),
        ],
        compiler_params=_cparams("parallel", "arbitrary"),
        name="dsa",
    )(iqt, dqt, iwt, ik, dk, dvt)


def _out_proj_kernel(*refs, n_in):
    res_ref = refs[0]
    a_refs = refs[1:1 + n_in]
    w_refs = refs[1 + n_in:1 + 2 * n_in]
    o_ref = refs[1 + 2 * n_in]
    acc = res_ref[...]
    for a, w in zip(a_refs, w_refs):
        acc = acc + _dot(a[...], w[...])
    o_ref[...] = acc


def _out_proj(res, acts, ws):
    m = res.shape[0]
    tm = PROJ_TM
    row = lambda n: pl.BlockSpec((tm, n), lambda i: (i, 0))
    return pl.pallas_call(
        functools.partial(_out_proj_kernel, n_in=len(acts)),
        grid=(m // tm,),
        in_specs=[row(D_MODEL)] + [row(a.shape[1]) for a in acts] + [_full(w.shape) for w in ws],
        out_specs=row(D_MODEL),
        out_shape=jax.ShapeDtypeStruct((m, D_MODEL), F32),
        compiler_params=_cparams("parallel"),
        name="out_proj",
    )(res, *acts, *ws)


def _ffn_kernel(h_ref, g_ref, wg_ref, wu_ref, wd_ref, fg_ref, o_ref, *, final_norm):
    x = h_ref[...]
    xn = _rms(x, g_ref[...]).astype(BF16)
    acc = x
    for a, b in FFN_CHUNKS:
        t = _silu(_dot(xn, wg_ref[:, a:b])) * _dot(xn, wu_ref[:, a:b])
        acc = acc + _dot(t.astype(BF16), wd_ref[a:b, :])
    o_ref[...] = _rms(acc, fg_ref[...]) if final_norm else acc


def _ffn(h, gain, wg, wu, wd, final_gain=None):
    m = h.shape[0]
    tm = FFN_TM
    fg = jnp.ones((1, D_MODEL), F32) if final_gain is None else final_gain.reshape(1, D_MODEL)
    row = pl.BlockSpec((tm, D_MODEL), lambda i: (i, 0))
    return pl.pallas_call(
        functools.partial(_ffn_kernel, final_norm=final_gain is not None),
        grid=(m // tm,),
        in_specs=[row, _full((1, D_MODEL)), _resident((D_MODEL, D_FF)), _resident((D_MODEL, D_FF)),
                  _resident((D_FF, D_MODEL)), _full((1, D_MODEL))],
        out_specs=row,
        out_shape=jax.ShapeDtypeStruct((m, D_MODEL), F32),
        compiler_params=_cparams("parallel"),
        name="ffn",
    )(h, gain.reshape(1, D_MODEL), wg.astype(BF16), wu.astype(BF16), wd.astype(BF16), fg)


def _odd_proj_kernel(x_ref, g_ref, w_ref, cos_ref, sin_ref, q_ref, k_ref, v_ref, gate_ref):
    xn = _rms(x_ref[...], g_ref[...]).astype(BF16)
    cos = cos_ref[...]
    sin = sin_ref[...]
    half = RET_DK // 2

    def rot(z, out_ref, scale):
        for h in range(RET_HEADS):
            x1 = z[:, h * RET_DK:h * RET_DK + half]
            x2 = z[:, h * RET_DK + half:(h + 1) * RET_DK]
            out_ref[:, h * RET_DK:h * RET_DK + half] = ((x1 * cos - x2 * sin) * scale).astype(BF16)
            out_ref[:, h * RET_DK + half:(h + 1) * RET_DK] = ((x2 * cos + x1 * sin) * scale).astype(BF16)

    nq = RET_HEADS * RET_DK
    nv = RET_HEADS * RET_DV
    rot(_dot(xn, w_ref[:, 0:nq]), q_ref, 1.0)
    rot(_dot(xn, w_ref[:, nq:2 * nq]), k_ref, RET_DK ** -0.5)
    v_ref[...] = _dot(xn, w_ref[:, 2 * nq:2 * nq + nv]).astype(BF16)
    gate_ref[...] = _dot(xn, w_ref[:, 2 * nq + nv:2 * nq + 2 * nv])


def _odd_proj(h, gain, w_in, t):
    m = h.shape[0]
    tm = PROJ_TM
    nt = t // tm
    half = RET_DK // 2
    inv = ROPE_BASE ** (-jnp.arange(half, dtype=F32) / half)
    ang = jnp.arange(t, dtype=jnp.int32).astype(F32)[:, None] * inv[None, :]
    row = lambda n: pl.BlockSpec((tm, n), lambda i: (i, 0))
    pos = pl.BlockSpec((tm, half), lambda i: (i % nt, 0))
    nq = RET_HEADS * RET_DK
    nv = RET_HEADS * RET_DV
    return pl.pallas_call(
        _odd_proj_kernel,
        grid=(m // tm,),
        in_specs=[row(D_MODEL), _full((1, D_MODEL)), _full(w_in.shape), pos, pos],
        out_specs=[row(nq), row(nq), row(nv), row(nv)],
        out_shape=[jax.ShapeDtypeStruct((m, nq), BF16), jax.ShapeDtypeStruct((m, nq), BF16),
                   jax.ShapeDtypeStruct((m, nv), BF16), jax.ShapeDtypeStruct((m, nv), F32)],
        compiler_params=_cparams("parallel"),
        name="odd_proj",
    )(h, gain.reshape(1, D_MODEL), w_in.astype(BF16), jnp.cos(ang), jnp.sin(ang))


def _ret_kernel(dec_ref, q_ref, k_ref, v_ref, gate_ref, dm_ref, xi_ref, ze_ref, rn_ref,
                o_ref, st_ref, *, n_chunks):
    c_ = RET_CHUNK

    @pl.when(pl.program_id(1) == 0)
    def _():
        st_ref[...] = jnp.zeros_like(st_ref)

    rn = rn_ref[...]
    for c in range(n_chunks):
        rows = slice(c * c_, (c + 1) * c_)
        for h in range(RET_HEADS):
            ks = slice(h * RET_DK, (h + 1) * RET_DK)
            vs = slice(h * RET_DV, (h + 1) * RET_DV)
            q = q_ref[rows, ks]
            k = k_ref[rows, ks]
            v = v_ref[rows, vs]
            st = st_ref[h]
            inner = (_dot_nt(q, k) * dm_ref[h]).astype(BF16)
            o = _dot(inner, v) + xi_ref[h] * _dot(q, st.astype(BF16))
            kz = (k.astype(F32) * ze_ref[h]).astype(BF16)
            st_ref[h] = dec_ref[h] * st + _dot_tn(kz, v)
            y = _rms(o, rn) * _silu(gate_ref[rows, vs])
            o_ref[rows, vs] = y.astype(o_ref.dtype)


def _retention(q, k, v, gate, rnorm, b, t):
    c_ = min(RET_CHUNK, t)
    tt = min(RET_TT, t)
    nt = t // tt
    log_g = jnp.log1p(-jnp.exp2(-5.0 - jnp.arange(RET_HEADS, dtype=F32)))
    idx = jnp.arange(c_, dtype=F32)
    rel = idx[:, None] - idx[None, :]
    dmat = jnp.where(rel >= 0, jnp.exp(log_g[:, None, None] * jnp.maximum(rel, 0.0)), 0.0)
    xi = jnp.exp(log_g[:, None] * (idx[None, :] + 1.0))[:, :, None]
    zeta = jnp.exp(log_g[:, None] * (c_ - 1.0 - idx[None, :]))[:, :, None]
    decay_c = jnp.exp(log_g * c_)
    nq = RET_HEADS * RET_DK
    nv = RET_HEADS * RET_DV
    row = lambda n: pl.BlockSpec((tt, n), lambda i, j: (i * nt + j, 0))
    return pl.pallas_call(
        functools.partial(_ret_kernel, n_chunks=tt // c_),
        grid=(b, nt),
        in_specs=[pl.BlockSpec(memory_space=pltpu.SMEM),
                  row(nq), row(nq), row(nv), row(nv),
                  _full((RET_HEADS, c_, c_)), _full((RET_HEADS, c_, 1)), _full((RET_HEADS, c_, 1)),
                  _full((1, RET_DV))],
        out_specs=row(nv),
        out_shape=jax.ShapeDtypeStruct((b * t, nv), BF16),
        scratch_shapes=[pltpu.VMEM((RET_HEADS, RET_DK, RET_DV), F32)],
        compiler_params=_cparams("parallel", "arbitrary"),
        name="retention",
    )(decay_c, q, k, v, gate, dmat, xi, zeta, rnorm.reshape(1, RET_DV))


def kernel(x, even_attn_norm, even_w_in, even_gla_wa2, even_gla_ba2, even_gla_norm, even_w_out,
           odd_attn_norm, odd_w_in, odd_ret_norm, odd_w_out,
           ffn_norm, ffn_w_gate, ffn_w_up, ffn_w_down, final_norm):
    b, t, d = x.shape
    h = x.reshape(b * t, d)

    gq, gk, gv, la, gr, dk, ik, dqt, iqt, dvt, iwt = _even_proj(
        h, even_attn_norm[0], even_w_in[0], even_gla_wa2[0], even_gla_ba2[0])
    o_gla = _gla(gq, gk, gv, la, gr, even_gla_norm[0], b, t)
    o_dsa = _dsa(iqt, dqt, iwt, ik, dk, dvt, b, t)
    w_out = even_w_out[0].astype(BF16)
    n_gla = GLA_HEADS * GLA_DV
    h = _out_proj(h, [o_gla, o_dsa], [w_out[:n_gla], w_out[n_gla:]])
    h = _ffn(h, ffn_norm[0], ffn_w_gate[0], ffn_w_up[0], ffn_w_down[0])

    q, k, v, gate = _odd_proj(h, odd_attn_norm[0], odd_w_in[0], t)
    o_ret = _retention(q, k, v, gate, odd_ret_norm[0], b, t)
    h = _out_proj(h, [o_ret], [odd_w_out[0].astype(BF16)])
    h = _ffn(h, ffn_norm[1], ffn_w_gate[1], ffn_w_up[1], ffn_w_down[1], final_gain=final_norm)
    return h.reshape(b, t, d)
```

```python
import functools

import jax
import jax.numpy as jnp
from jax import lax
from jax.experimental import pallas as pl
from jax.experimental.pallas import tpu as pltpu

F32 = jnp.float32
BF16 = jnp.bfloat16

D_MODEL = 1024
EPS = 1e-6
GLA_HEADS = 4
GLA_DV = 128
GLA_DK = 64
GLA_RANK = 16
GLA_GATE_NORM = 16.0
GLA_CHUNK = 64
DSA_HEADS = 4
DSA_DH = 128
IDX_HEADS = 8
IDX_DIM = 64
TOPK_MAX = 256
RET_HEADS = 4
RET_DK = 256
RET_DV = 512
ROPE_BASE = 10000.0
D_FF = 2816

LANES = 128
SUBLANES = 8
VMEM_LIMIT = 56 * 1024 * 1024

PROJ_TM = 512
FFN_TM = 512
FFN_CHUNKS = ((0, 1024), (1024, 2048), (2048, 2816))
GLA_TT = 512
RET_CHUNK = 256
RET_TT = 512
DSA_TQ = 256
DSA_TK = 256
DSA_SK = 128
DSA_VROWS = DSA_DH + 16
LOG2E = 1.4426950408889634
PACKED_ROWS = 16
BISECT16_ITERS = 8
BISECT_ITERS = 8


def _cparams(*sem):
    return pltpu.CompilerParams(dimension_semantics=sem, vmem_limit_bytes=VMEM_LIMIT)


def _dot(a, b):
    return jnp.dot(a, b, preferred_element_type=F32)


def _dot_nt(a, b):
    return lax.dot_general(a, b, (((1,), (1,)), ((), ())), preferred_element_type=F32)


def _dot_tn(a, b):
    return lax.dot_general(a, b, (((0,), (0,)), ((), ())), preferred_element_type=F32)


def _rms(x, g):
    return x * lax.rsqrt(jnp.mean(x * x, axis=-1, keepdims=True) + EPS) * g


def _silu(x):
    return x * jax.nn.sigmoid(x)


def _full(shape):
    return pl.BlockSpec(shape, lambda *_: (0,) * len(shape))


def _resident(shape):
    return pl.BlockSpec(shape, lambda *_: (0,) * len(shape), pipeline_mode=pl.Buffered(1))


_E_GQ, _E_GK, _E_GV, _E_GA, _E_GR, _E_DK, _E_IK, _E_END = (0, 256, 512, 1024, 1152, 1664, 1792, 1920)
_T_DQ, _T_IQ, _T_DV, _T_IW, _T_END = (0, 512, 1024, 1152, 1168)


def _even_proj_kernel(x_ref, g_ref, w_ref, wt_ref, wa2_ref, ba2_ref,
                      gq_ref, gk_ref, gv_ref, la_ref, gr_ref, dk_ref, ik_ref,
                      dqt_ref, iqt_ref, dvt_ref, iwt_ref):
    xn = _rms(x_ref[...], g_ref[...]).astype(BF16)

    def proj(a, b):
        return _dot(xn, w_ref[:, a:b])

    def proj_t(a, b):
        return _dot_nt(wt_ref[a:b, :], xn)

    gq_ref[...] = proj(_E_GQ, _E_GK)
    gk_ref[...] = proj(_E_GK, _E_GV)
    gv_ref[...] = proj(_E_GV, _E_GA).astype(BF16)
    ga = proj(_E_GA, _E_GR).astype(BF16)
    z = _dot(ga, wa2_ref[...]) + ba2_ref[...]
    la_ref[...] = jax.nn.log_sigmoid(z) * (1.0 / GLA_GATE_NORM)
    gr_ref[...] = proj(_E_GR, _E_DK)
    dk_ref[...] = proj(_E_DK, _E_IK).astype(BF16)
    ik_ref[...] = proj(_E_IK, _E_IK + IDX_DIM).astype(BF16)
    dqt_ref[...] = (proj_t(_T_DQ, _T_IQ) * (DSA_DH ** -0.5 * LOG2E)).astype(BF16)
    iqt_ref[...] = proj_t(_T_IQ, _T_DV).astype(BF16)
    dvt = proj_t(_T_DV, _T_IW).astype(BF16)
    for c in range(dvt_ref.shape[0]):
        dvt_ref[c, 0:DSA_DH, :] = dvt[:, c * DSA_TK:(c + 1) * DSA_TK]
        dvt_ref[c, DSA_DH:DSA_VROWS, :] = jnp.ones((DSA_VROWS - DSA_DH, DSA_TK), BF16)
    iwt_ref[...] = proj_t(_T_IW, _T_END) * ((IDX_HEADS ** -0.5) * (IDX_DIM ** -0.5))


def _even_proj(x2, gain, w_in, wa2, ba2):
    m = x2.shape[0]
    tm = PROJ_TM
    s = [0, 256, 512, 1024, 1040, 1552, 2064, 2192, 2320, 2832, 2896, 2904]
    gq, gk, gv, ga, gr, dq, dk, dv, iq, ik, iw = [w_in[:, s[i]:s[i + 1]] for i in range(11)]
    zpad = lambda n: jnp.zeros((D_MODEL, n), w_in.dtype)
    w = jnp.concatenate([gq, gk, gv, ga, zpad(LANES - GLA_RANK), gr, dk, ik, zpad(LANES - IDX_DIM)],
                        axis=1).astype(BF16)
    wt = jnp.concatenate([dq, iq, dv, iw, zpad(_T_END - _T_IW - IDX_HEADS)], axis=1).T.astype(BF16)
    wa2p = jnp.concatenate(
        [wa2, jnp.zeros((LANES - GLA_RANK, wa2.shape[1]), wa2.dtype)], axis=0).astype(BF16)
    row = lambda n: pl.BlockSpec((tm, n), lambda i: (i, 0))
    col = lambda n: pl.BlockSpec((n, tm), lambda i: (0, i))
    n_iw = _T_END - _T_IW
    return pl.pallas_call(
        _even_proj_kernel,
        grid=(m // tm,),
        in_specs=[row(D_MODEL), _full((1, D_MODEL)), _full((D_MODEL, _E_END)),
                  _full((_T_END, D_MODEL)), _full((LANES, 256)), _full((1, 256))],
        out_specs=[row(256), row(256), row(512), row(256), row(512), row(DSA_DH), row(IDX_DIM),
                   col(512), col(512),
                   pl.BlockSpec((tm // DSA_TK, DSA_VROWS, DSA_TK), lambda i: (i, 0, 0)),
                   col(n_iw)],
        out_shape=[jax.ShapeDtypeStruct((m, 256), F32), jax.ShapeDtypeStruct((m, 256), F32),
                   jax.ShapeDtypeStruct((m, 512), BF16), jax.ShapeDtypeStruct((m, 256), F32),
                   jax.ShapeDtypeStruct((m, 512), F32), jax.ShapeDtypeStruct((m, DSA_DH), BF16),
                   jax.ShapeDtypeStruct((m, IDX_DIM), BF16),
                   jax.ShapeDtypeStruct((512, m), BF16), jax.ShapeDtypeStruct((512, m), BF16),
                   jax.ShapeDtypeStruct((m // DSA_TK, DSA_VROWS, DSA_TK), BF16),
                   jax.ShapeDtypeStruct((n_iw, m), F32)],
        compiler_params=_cparams("parallel"),
        name="even_proj",
    )(x2, gain.reshape(1, D_MODEL), w, wt, wa2p, ba2.reshape(1, 256))


def _gla_kernel(q_ref, k_ref, v_ref, la_ref, gr_ref, gn_ref, o_ref, st_ref, *, n_chunks):
    c_ = GLA_CHUNK

    @pl.when(pl.program_id(1) == 0)
    def _():
        st_ref[...] = jnp.zeros_like(st_ref)

    ri = lax.broadcasted_iota(jnp.int32, (c_, c_), 0)
    ci = lax.broadcasted_iota(jnp.int32, (c_, c_), 1)
    tri = ri >= ci
    ltri = jnp.where(tri, 1.0, 0.0).astype(BF16)
    gn = gn_ref[...]

    chunks = range(n_chunks)
    heads = range(GLA_HEADS)
    rows = [slice(c * c_, (c + 1) * c_) for c in chunks]
    vs = [slice(h * GLA_DV, (h + 1) * GLA_DV) for h in heads]

    def cumsum(g):
        g1 = g.astype(BF16)
        r1 = g - g1.astype(F32)
        g2 = r1.astype(BF16)
        g3 = (r1 - g2.astype(F32)).astype(BF16)
        return _dot(ltri, g1) + _dot(ltri, g2) + _dot(ltri, g3)

    low_half = lax.broadcasted_iota(jnp.int32, (c_, LANES), 1) < GLA_DK
    pair = [slice((h // 2) * LANES, (h // 2 + 1) * LANES) for h in heads]

    def half(z, h):
        zp = z[:, pair[h]]
        return (jnp.where(low_half, zp, 0.0) if h % 2 == 0 else jnp.where(low_half, 0.0, zp)).astype(BF16)

    cum = [cumsum(la_ref[rows[c], :]) for c in chunks]
    last = [cum[c][c_ - 1:c_, :] for c in chunks]
    q_t = [(q_ref[rows[c], :] * jnp.exp(cum[c]) * (GLA_DK ** -0.5)).astype(BF16) for c in chunks]
    k_t = [k_ref[rows[c], :] * jnp.exp(-cum[c]) for c in chunks]
    k_e = [k_ref[rows[c], :] * jnp.exp(last[c] - cum[c]) for c in chunks]
    dec = [jnp.exp(last[c]) for c in chunks]
    a = [[jnp.where(tri, _dot_nt(q_t[c][:, pair[h]], half(k_t[c], h)), 0.0).astype(BF16)
          for h in heads] for c in chunks]
    o_intra = [[_dot(a[c][h], v_ref[rows[c], vs[h]]) for h in heads] for c in chunks]
    kv = [[_dot_tn(v_ref[rows[c], vs[h]], half(k_e[c], h)) for h in heads] for c in chunks]
    for h in heads:
        st = st_ref[h]
        for c in chunks:
            o = o_intra[c][h] + _dot_nt(q_t[c][:, pair[h]], st.astype(BF16))
            st = dec[c][:, pair[h]] * st + kv[c][h]
            y = _rms(o, gn) * _silu(gr_ref[rows[c], vs[h]])
            o_ref[rows[c], vs[h]] = y.astype(o_ref.dtype)
        st_ref[h] = st


def _gla(gq, gk, gv, la, gr, gnorm, b, t):
    tt = min(GLA_TT, t)
    nt = t // tt
    row = lambda n: pl.BlockSpec((tt, n), lambda i, j: (i * nt + j, 0))
    return pl.pallas_call(
        functools.partial(_gla_kernel, n_chunks=tt // GLA_CHUNK),
        grid=(b, nt),
        in_specs=[row(256), row(256), row(512), row(256), row(512), _full((1, GLA_DV))],
        out_specs=row(512),
        out_shape=jax.ShapeDtypeStruct((b * t, 512), BF16),
        scratch_shapes=[pltpu.VMEM((GLA_HEADS, GLA_DV, LANES), F32)],
        compiler_params=_cparams("parallel", "arbitrary"),
        name="gla",
    )(gq, gk, gv, la, gr, gnorm.reshape(1, GLA_DV))


def _dsa_kernel(iqt_ref, dqt_ref, iwt_ref, ik_ref, dk_ref, dvt_ref, o_ref,
                s_ref, sb_ref, acc_ref, bias_ref, sbuf_ref, *, topk, t_len):
    tq, tk, sk = DSA_TQ, DSA_TK, DSA_SK
    n_sub = tk // sk
    qi = pl.program_id(1)
    q0 = qi * tq
    nkt = (q0 + tq + tk - 1) // tk
    neg_inf = -jnp.inf

    qpos = q0 + lax.broadcasted_iota(jnp.int32, (sk, tq), 1)
    krow = lax.broadcasted_iota(jnp.int32, (sk, tq), 0)
    krow_t = lax.broadcasted_iota(jnp.int32, (tk, tq), 0)

    def colsum(c):
        r = c.shape[0] // (4 * SUBLANES)
        part = jnp.sum(c.reshape(r, 4, SUBLANES, tq), axis=0)
        return jnp.sum(part, axis=0)

    def score_tile(kt, carry, diagonal):
        rmax, rmin = carry
        for j in range(n_sub):
            k0 = pl.multiple_of(kt * tk, tk) + j * sk
            ik_t = ik_ref[pl.ds(k0, sk), :]
            sc = jnp.zeros((sk, tq), F32)
            for h in range(IDX_HEADS):
                lg = _dot(ik_t, iqt_ref[h * IDX_DIM:(h + 1) * IDX_DIM, :])
                sc = sc + iwt_ref[h:h + 1, :] * jnp.maximum(lg, 0.0)
            if diagonal:
                allowed = (k0 + krow) <= qpos
                hi_part, lo_part = jnp.where(allowed, sc, neg_inf), jnp.where(allowed, sc, jnp.inf)
            else:
                hi_part, lo_part = sc, sc
            s_ref[kt, j * sk:(j + 1) * sk, :] = hi_part
            sb_ref[kt, j * sk:(j + 1) * sk, :] = hi_part.astype(BF16)
            rmax = jnp.maximum(rmax, jnp.max(hi_part.reshape(sk // SUBLANES, SUBLANES, tq), axis=0))
            rmin = jnp.minimum(rmin, jnp.min(lo_part.reshape(sk // SUBLANES, SUBLANES, tq), axis=0))
        return rmax, rmin

    def score_pair(i, carry):
        carry = score_tile(2 * i, carry, False)
        return score_tile(jnp.minimum(2 * i + 1, nkt - 2), carry, False)

    carry = lax.fori_loop(0, nkt // 2, score_pair,
                          (jnp.full((SUBLANES, tq), neg_inf, F32), jnp.full((SUBLANES, tq), jnp.inf, F32)))
    rmax, rmin = score_tile(nkt - 1, carry, True)
    rmax = jnp.max(rmax, axis=0, keepdims=True)
    rmin = jnp.min(rmin, axis=0, keepdims=True)

    def over_tiles(body, init):
        carry = lax.fori_loop(0, nkt // 2, lambda i, c: body(2 * i + 1, body(2 * i, c)), init)
        return lax.cond(nkt % 2 == 1, lambda: body(nkt - 1, carry), lambda: carry)

    def count(ind_fn):
        def body(kt, acc):
            return acc + colsum(ind_fn(s_ref[kt], kt))
        acc = over_tiles(body, jnp.zeros((SUBLANES, tq), F32))
        return jnp.sum(acc, axis=0, keepdims=True)

    kf = jnp.float32(topk)

    def count16(mid_b):
        def body(kt, acc):
            ind = jnp.where(sb_ref[kt] >= mid_b, jnp.ones((), BF16), jnp.zeros((), BF16))
            parts = [ind[r * PACKED_ROWS:(r + 1) * PACKED_ROWS, :] for r in range(tk // PACKED_ROWS)]
            while len(parts) > 1:
                parts = [parts[i] + parts[i + 1] for i in range(0, len(parts), 2)]
            return acc + parts[0].astype(F32)
        acc = over_tiles(body, jnp.zeros((PACKED_ROWS, tq), F32))
        return jnp.sum(acc, axis=0, keepdims=True)

    def bisect16(_, carry):
        lo, hi = carry
        mid_b = (lo + (hi - lo) * 0.5).astype(BF16)
        mid = mid_b.astype(F32)
        ge = count16(mid_b) >= kf
        below = mid - jnp.maximum(jnp.abs(mid) * 2.0 ** -7, 1e-30)
        return (jnp.where(ge, jnp.maximum(lo, below), lo), jnp.where(ge, hi, jnp.minimum(hi, mid)))

    def bisect(_, carry):
        lo, hi = carry
        mid = lo + (hi - lo) * 0.5
        ge = count(lambda s, kt: jnp.where(s >= mid, 1.0, 0.0)) >= kf
        return jnp.where(ge, mid, lo), jnp.where(ge, hi, mid)

    lo, hi = lax.fori_loop(0, BISECT16_ITERS, bisect16, (rmin, rmax))
    lo, _ = lax.fori_loop(0, BISECT_ITERS, bisect, (lo, hi))

    def colmin(c):
        return jnp.min(c.reshape(tk // SUBLANES, SUBLANES, tq), axis=0)

    def snap(kt, carry):
        cnt, mn = carry
        s = s_ref[kt]
        ge = s >= lo
        return cnt + colsum(jnp.where(ge, 1.0, 0.0)), jnp.minimum(mn, colmin(jnp.where(ge, s, jnp.inf)))

    c_lo, v0 = over_tiles(snap, (jnp.zeros((SUBLANES, tq), F32), jnp.full((SUBLANES, tq), jnp.inf, F32)))
    c_lo = jnp.sum(c_lo, axis=0, keepdims=True)
    v0 = jnp.min(v0, axis=0, keepdims=True)

    def above(v):
        def body(kt, carry):
            cnt, mn = carry
            s = s_ref[kt]
            gt = s > v
            return (cnt + colsum(jnp.where(gt, 1.0, 0.0)),
                    jnp.minimum(mn, colmin(jnp.where(gt, s, jnp.inf))))
        cnt, mn = over_tiles(body, (jnp.zeros((SUBLANES, tq), F32), jnp.full((SUBLANES, tq), jnp.inf, F32)))
        return jnp.sum(cnt, axis=0, keepdims=True), jnp.min(mn, axis=0, keepdims=True)

    def step_up(carry):
        v, f_v, _, _ = carry
        n_gt, v_next = above(v)
        fin = n_gt < kf
        pending = jnp.max(jnp.where(fin, 0.0, 1.0))
        return jnp.where(fin, v, v_next), jnp.where(fin, f_v, n_gt), n_gt, pending

    thr, f_thr, n_gt, _ = lax.while_loop(
        lambda carry: carry[3] > 0.0, step_up, (v0, c_lo, jnp.zeros((1, tq), F32), jnp.float32(1.0)))
    n_eq = f_thr - n_gt
    need = kf - n_gt

    @pl.when(jnp.max(n_eq - need) > 0.0)
    def _():
        def ibisect(_, carry):
            ilo, ihi = carry
            imid = jnp.right_shift(ilo + ihi, 1)
            c = count(lambda s, kt: jnp.where(
                s == thr, jnp.where((kt * tk + krow_t) <= imid, 1.0, 0.0), 0.0))
            ge = c >= need
            return jnp.where(ge, ilo, imid), jnp.where(ge, imid, ihi)
        ilo0 = jnp.full((1, tq), -1, jnp.int32)
        ihi0 = jnp.full((1, tq), t_len - 1, jnp.int32)
        n_it = max(1, int(t_len - 1).bit_length() + 1)
        jlim = lax.fori_loop(0, n_it, ibisect, (ilo0, ihi0))[1]

        def drop(kt, carry):
            s = s_ref[kt]
            s_ref[kt] = jnp.where(s == thr, jnp.where((kt * tk + krow_t) <= jlim, s, neg_inf), s)
            return carry
        lax.fori_loop(0, nkt, drop, 0)

    acc_ref[...] = jnp.zeros(acc_ref.shape, F32)

    def logits(t, slot):
        tc = jnp.minimum(t, nkt - 1)
        thr_eff = jnp.where(t < nkt, thr, jnp.inf)
        bias_ref[slot] = jnp.where(s_ref[tc] >= thr_eff, 0.0, neg_inf)
        k_t = dk_ref[pl.ds(pl.multiple_of(tc * tk, tk), tk), :]
        for h in range(DSA_HEADS):
            sbuf_ref[slot, h] = _dot(k_t, dqt_ref[h * DSA_DH:(h + 1) * DSA_DH, :]) + bias_ref[slot]

    def softmax_pv(t, slot, ms):
        ms = list(ms)
        vt_t = dvt_ref[jnp.minimum(t, nkt - 1)]
        for h in range(DSA_HEADS):
            m_new = jnp.maximum(ms[h], jnp.max(sbuf_ref[slot, h], axis=0, keepdims=True))
            m_safe = jnp.where(m_new == neg_inf, 0.0, m_new)
            alpha = jnp.exp2(ms[h] - m_safe)
            p = jnp.exp2(sbuf_ref[slot, h] - m_safe).astype(BF16)
            acc_ref[h] = alpha * acc_ref[h] + _dot(vt_t, p)
            ms[h] = m_new
        return tuple(ms)

    def attend(i, ms):
        logits(2 * i + 1, 1)
        ms = softmax_pv(2 * i, 0, ms)
        logits(2 * i + 2, 0)
        return softmax_pv(2 * i + 1, 1, ms)

    logits(0, 0)
    m0 = tuple(jnp.full((1, tq), neg_inf, F32) for _ in range(DSA_HEADS))
    lax.fori_loop(0, (nkt + 1) // 2, attend, m0)
    for h in range(DSA_HEADS):
        num = acc_ref[h, 0:DSA_DH, :]
        den = acc_ref[h, DSA_DH:DSA_DH + 1, :]
        o_ref[:, h * DSA_DH:(h + 1) * DSA_DH] = (num / den).T.astype(o_ref.dtype)


def _dsa(iqt, dqt, iwt, ik, dk, dvt, b, t):
    tq, tk, sk = DSA_TQ, DSA_TK, DSA_SK
    assert tq == tk and t % tq == 0
    nq = t // tq
    topk = min(TOPK_MAX, t // 4)
    qcol = lambda n: pl.BlockSpec((n, tq), lambda i, j: (0, i * nq + j))
    krow = lambda n: pl.BlockSpec((t, n), lambda i, j: (i, 0))
    return pl.pallas_call(
        functools.partial(_dsa_kernel, topk=topk, t_len=t),
        grid=(b, nq),
        in_specs=[qcol(512), qcol(512), qcol(iwt.shape[0]), krow(IDX_DIM), krow(DSA_DH),
                  pl.BlockSpec((t // tk, DSA_VROWS, tk), lambda i, j: (i, 0, 0))],
        out_specs=pl.BlockSpec((tq, 512), lambda i, j: (i * nq + j, 0)),
        out_shape=jax.ShapeDtypeStruct((b * t, 512), BF16),
        scratch_shapes=[
            pltpu.VMEM((t // tk, tk, tq), F32),
            pltpu.VMEM((t // tk, tk, tq), BF16),
            pltpu.VMEM((DSA_HEADS, DSA_VROWS, tq), F32),
            pltpu.VMEM((2, tk, tq), F32),
            pltpu.VMEM((2, DSA_HEADS, tk, tq), F32),
        ],
        compiler_params=_cparams("parallel", "arbitrary"),
        name="dsa",
    )(iqt, dqt, iwt, ik, dk, dvt)


def _out_proj_kernel(*refs, n_in):
    res_ref = refs[0]
    a_refs = refs[1:1 + n_in]
    w_refs = refs[1 + n_in:1 + 2 * n_in]
    o_ref = refs[1 + 2 * n_in]
    acc = res_ref[...]
    for a, w in zip(a_refs, w_refs):
        acc = acc + _dot(a[...], w[...])
    o_ref[...] = acc


def _out_proj(res, acts, ws):
    m = res.shape[0]
    tm = PROJ_TM
    row = lambda n: pl.BlockSpec((tm, n), lambda i: (i, 0))
    return pl.pallas_call(
        functools.partial(_out_proj_kernel, n_in=len(acts)),
        grid=(m // tm,),
        in_specs=[row(D_MODEL)] + [row(a.shape[1]) for a in acts] + [_full(w.shape) for w in ws],
        out_specs=row(D_MODEL),
        out_shape=jax.ShapeDtypeStruct((m, D_MODEL), F32),
        compiler_params=_cparams("parallel"),
        name="out_proj",
    )(res, *acts, *ws)


def _ffn_kernel(h_ref, g_ref, wg_ref, wu_ref, wd_ref, fg_ref, o_ref, *, final_norm):
    x = h_ref[...]
    xn = _rms(x, g_ref[...]).astype(BF16)
    acc = x
    for a, b in FFN_CHUNKS:
        t = _silu(_dot(xn, wg_ref[:, a:b])) * _dot(xn, wu_ref[:, a:b])
        acc = acc + _dot(t.astype(BF16), wd_ref[a:b, :])
    o_ref[...] = _rms(acc, fg_ref[...]) if final_norm else acc


def _ffn(h, gain, wg, wu, wd, final_gain=None):
    m = h.shape[0]
    tm = FFN_TM
    fg = jnp.ones((1, D_MODEL), F32) if final_gain is None else final_gain.reshape(1, D_MODEL)
    row = pl.BlockSpec((tm, D_MODEL), lambda i: (i, 0))
    return pl.pallas_call(
        functools.partial(_ffn_kernel, final_norm=final_gain is not None),
        grid=(m // tm,),
        in_specs=[row, _full((1, D_MODEL)), _resident((D_MODEL, D_FF)), _resident((D_MODEL, D_FF)),
                  _resident((D_FF, D_MODEL)), _full((1, D_MODEL))],
        out_specs=row,
        out_shape=jax.ShapeDtypeStruct((m, D_MODEL), F32),
        compiler_params=_cparams("parallel"),
        name="ffn",
    )(h, gain.reshape(1, D_MODEL), wg.astype(BF16), wu.astype(BF16), wd.astype(BF16), fg)


def _odd_proj_kernel(x_ref, g_ref, w_ref, cos_ref, sin_ref, q_ref, k_ref, v_ref, gate_ref):
    xn = _rms(x_ref[...], g_ref[...]).astype(BF16)
    cos = cos_ref[...]
    sin = sin_ref[...]
    half = RET_DK // 2

    def rot(z, out_ref, scale):
        for h in range(RET_HEADS):
            x1 = z[:, h * RET_DK:h * RET_DK + half]
            x2 = z[:, h * RET_DK + half:(h + 1) * RET_DK]
            out_ref[:, h * RET_DK:h * RET_DK + half] = ((x1 * cos - x2 * sin) * scale).astype(BF16)
            out_ref[:, h * RET_DK + half:(h + 1) * RET_DK] = ((x2 * cos + x1 * sin) * scale).astype(BF16)

    nq = RET_HEADS * RET_DK
    nv = RET_HEADS * RET_DV
    rot(_dot(xn, w_ref[:, 0:nq]), q_ref, 1.0)
    rot(_dot(xn, w_ref[:, nq:2 * nq]), k_ref, RET_DK ** -0.5)
    v_ref[...] = _dot(xn, w_ref[:, 2 * nq:2 * nq + nv]).astype(BF16)
    gate_ref[...] = _dot(xn, w_ref[:, 2 * nq + nv:2 * nq + 2 * nv])


def _odd_proj(h, gain, w_in, t):
    m = h.shape[0]
    tm = PROJ_TM
    nt = t // tm
    half = RET_DK // 2
    inv = ROPE_BASE ** (-jnp.arange(half, dtype=F32) / half)
    ang = jnp.arange(t, dtype=jnp.int32).astype(F32)[:, None] * inv[None, :]
    row = lambda n: pl.BlockSpec((tm, n), lambda i: (i, 0))
    pos = pl.BlockSpec((tm, half), lambda i: (i % nt, 0))
    nq = RET_HEADS * RET_DK
    nv = RET_HEADS * RET_DV
    return pl.pallas_call(
        _odd_proj_kernel,
        grid=(m // tm,),
        in_specs=[row(D_MODEL), _full((1, D_MODEL)), _full(w_in.shape), pos, pos],
        out_specs=[row(nq), row(nq), row(nv), row(nv)],
        out_shape=[jax.ShapeDtypeStruct((m, nq), BF16), jax.ShapeDtypeStruct((m, nq), BF16),
                   jax.ShapeDtypeStruct((m, nv), BF16), jax.ShapeDtypeStruct((m, nv), F32)],
        compiler_params=_cparams("parallel"),
        name="odd_proj",
    )(h, gain.reshape(1, D_MODEL), w_in.astype(BF16), jnp.cos(ang), jnp.sin(ang))


def _ret_kernel(dec_ref, q_ref, k_ref, v_ref, gate_ref, dm_ref, xi_ref, ze_ref, rn_ref,
                o_ref, st_ref, *, n_chunks):
    c_ = RET_CHUNK

    @pl.when(pl.program_id(1) == 0)
    def _():
        st_ref[...] = jnp.zeros_like(st_ref)

    rn = rn_ref[...]
    for c in range(n_chunks):
        rows = slice(c * c_, (c + 1) * c_)
        for h in range(RET_HEADS):
            ks = slice(h * RET_DK, (h + 1) * RET_DK)
            vs = slice(h * RET_DV, (h + 1) * RET_DV)
            q = q_ref[rows, ks]
            k = k_ref[rows, ks]
            v = v_ref[rows, vs]
            st = st_ref[h]
            inner = (_dot_nt(q, k) * dm_ref[h]).astype(BF16)
            o = _dot(inner, v) + xi_ref[h] * _dot(q, st.astype(BF16))
            kz = (k.astype(F32) * ze_ref[h]).astype(BF16)
            st_ref[h] = dec_ref[h] * st + _dot_tn(kz, v)
            y = _rms(o, rn) * _silu(gate_ref[rows, vs])
            o_ref[rows, vs] = y.astype(o_ref.dtype)


def _retention(q, k, v, gate, rnorm, b, t):
    c_ = min(RET_CHUNK, t)
    tt = min(RET_TT, t)
    nt = t // tt
    log_g = jnp.log1p(-jnp.exp2(-5.0 - jnp.arange(RET_HEADS, dtype=F32)))
    idx = jnp.arange(c_, dtype=F32)
    rel = idx[:, None] - idx[None, :]
    dmat = jnp.where(rel >= 0, jnp.exp(log_g[:, None, None] * jnp.maximum(rel, 0.0)), 0.0)
    xi = jnp.exp(log_g[:, None] * (idx[None, :] + 1.0))[:, :, None]
    zeta = jnp.exp(log_g[:, None] * (c_ - 1.0 - idx[None, :]))[:, :, None]
    decay_c = jnp.exp(log_g * c_)
    nq = RET_HEADS * RET_DK
    nv = RET_HEADS * RET_DV
    row = lambda n: pl.BlockSpec((tt, n), lambda i, j: (i * nt + j, 0))
    return pl.pallas_call(
        functools.partial(_ret_kernel, n_chunks=tt // c_),
        grid=(b, nt),
        in_specs=[pl.BlockSpec(memory_space=pltpu.SMEM),
                  row(nq), row(nq), row(nv), row(nv),
                  _full((RET_HEADS, c_, c_)), _full((RET_HEADS, c_, 1)), _full((RET_HEADS, c_, 1)),
                  _full((1, RET_DV))],
        out_specs=row(nv),
        out_shape=jax.ShapeDtypeStruct((b * t, nv), BF16),
        scratch_shapes=[pltpu.VMEM((RET_HEADS, RET_DK, RET_DV), F32)],
        compiler_params=_cparams("parallel", "arbitrary"),
        name="retention",
    )(decay_c, q, k, v, gate, dmat, xi, zeta, rnorm.reshape(1, RET_DV))


def kernel(x, even_attn_norm, even_w_in, even_gla_wa2, even_gla_ba2, even_gla_norm, even_w_out,
           odd_attn_norm, odd_w_in, odd_ret_norm, odd_w_out,
           ffn_norm, ffn_w_gate, ffn_w_up, ffn_w_down, final_norm):
    b, t, d = x.shape
    h = x.reshape(b * t, d)

    gq, gk, gv, la, gr, dk, ik, dqt, iqt, dvt, iwt = _even_proj(
        h, even_attn_norm[0], even_w_in[0], even_gla_wa2[0], even_gla_ba2[0])
    o_gla = _gla(gq, gk, gv, la, gr, even_gla_norm[0], b, t)
    o_dsa = _dsa(iqt, dqt, iwt, ik, dk, dvt, b, t)
    w_out = even_w_out[0].astype(BF16)
    n_gla = GLA_HEADS * GLA_DV
    h = _out_proj(h, [o_gla, o_dsa], [w_out[:n_gla], w_out[n_gla:]])
    h = _ffn(h, ffn_norm[0], ffn_w_gate[0], ffn_w_up[0], ffn_w_down[0])

    q, k, v, gate = _odd_proj(h, odd_attn_norm[0], odd_w_in[0], t)
    o_ret = _retention(q, k, v, gate, odd_ret_norm[0], b, t)
    h = _out_proj(h, [o_ret], [odd_w_out[0].astype(BF16)])
    h = _ffn(h, ffn_norm[1], ffn_w_gate[1], ffn_w_up[1], ffn_w_down[1], final_gain=final_norm)
    return h.reshape(b, t, d)
```

```python
import functools

import jax
import jax.numpy as jnp
from jax import lax
from jax.experimental import pallas as pl
from jax.experimental.pallas import tpu as pltpu

F32 = jnp.float32
BF16 = jnp.bfloat16

D_MODEL = 1024
EPS = 1e-6
GLA_HEADS = 4
GLA_DV = 128
GLA_DK = 64
GLA_RANK = 16
GLA_GATE_NORM = 16.0
GLA_CHUNK = 64
DSA_HEADS = 4
DSA_DH = 128
IDX_HEADS = 8
IDX_DIM = 64
TOPK_MAX = 256
RET_HEADS = 4
RET_DK = 256
RET_DV = 512
ROPE_BASE = 10000.0
D_FF = 2816

LANES = 128
SUBLANES = 8
VMEM_LIMIT = 56 * 1024 * 1024

PROJ_TM = 512
FFN_TM = 512
FFN_CHUNKS = ((0, 1024), (1024, 2048), (2048, 2816))
GLA_TT = 512
RET_CHUNK = 256
RET_TT = 512
DSA_TQ = 256
DSA_TK = 256
DSA_SK = 128
DSA_VROWS = DSA_DH + 16
LOG2E = 1.4426950408889634
PACKED_ROWS = 16
BISECT16_ITERS = 8
BISECT_ITERS = 8


def _cparams(*sem):
    return pltpu.CompilerParams(dimension_semantics=sem, vmem_limit_bytes=VMEM_LIMIT)


def _dot(a, b):
    return jnp.dot(a, b, preferred_element_type=F32)


def _dot_nt(a, b):
    return lax.dot_general(a, b, (((1,), (1,)), ((), ())), preferred_element_type=F32)


def _dot_tn(a, b):
    return lax.dot_general(a, b, (((0,), (0,)), ((), ())), preferred_element_type=F32)


def _rms(x, g):
    return x * lax.rsqrt(jnp.mean(x * x, axis=-1, keepdims=True) + EPS) * g


def _silu(x):
    return x * jax.nn.sigmoid(x)


def _full(shape):
    return pl.BlockSpec(shape, lambda *_: (0,) * len(shape))


def _resident(shape):
    return pl.BlockSpec(shape, lambda *_: (0,) * len(shape), pipeline_mode=pl.Buffered(1))


_E_GQ, _E_GK, _E_GV, _E_GA, _E_GR, _E_DK, _E_IK, _E_END = (0, 256, 512, 1024, 1152, 1664, 1792, 1920)
_T_DQ, _T_IQ, _T_DV, _T_IW, _T_END = (0, 512, 1024, 1152, 1168)


def _even_proj_kernel(x_ref, g_ref, w_ref, wt_ref, wa2_ref, ba2_ref,
                      gq_ref, gk_ref, gv_ref, la_ref, gr_ref, dk_ref, ik_ref,
                      dqt_ref, iqt_ref, dvt_ref, iwt_ref):
    xn = _rms(x_ref[...], g_ref[...]).astype(BF16)

    def proj(a, b):
        return _dot(xn, w_ref[:, a:b])

    def proj_t(a, b):
        return _dot_nt(wt_ref[a:b, :], xn)

    gq_ref[...] = proj(_E_GQ, _E_GK)
    gk_ref[...] = proj(_E_GK, _E_GV)
    gv_ref[...] = proj(_E_GV, _E_GA).astype(BF16)
    ga = proj(_E_GA, _E_GR).astype(BF16)
    z = _dot(ga, wa2_ref[...]) + ba2_ref[...]
    la_ref[...] = jax.nn.log_sigmoid(z) * (1.0 / GLA_GATE_NORM)
    gr_ref[...] = proj(_E_GR, _E_DK)
    dk_ref[...] = proj(_E_DK, _E_IK).astype(BF16)
    ik_ref[...] = proj(_E_IK, _E_IK + IDX_DIM).astype(BF16)
    dqt_ref[...] = (proj_t(_T_DQ, _T_IQ) * (DSA_DH ** -0.5 * LOG2E)).astype(BF16)
    iqt_ref[...] = proj_t(_T_IQ, _T_DV).astype(BF16)
    dvt = proj_t(_T_DV, _T_IW).astype(BF16)
    for c in range(dvt_ref.shape[0]):
        dvt_ref[c, 0:DSA_DH, :] = dvt[:, c * DSA_TK:(c + 1) * DSA_TK]
        dvt_ref[c, DSA_DH:DSA_VROWS, :] = jnp.ones((DSA_VROWS - DSA_DH, DSA_TK), BF16)
    iwt_ref[...] = proj_t(_T_IW, _T_END) * ((IDX_HEADS ** -0.5) * (IDX_DIM ** -0.5))


def _even_proj(x2, gain, w_in, wa2, ba2):
    m = x2.shape[0]
    tm = PROJ_TM
    s = [0, 256, 512, 1024, 1040, 1552, 2064, 2192, 2320, 2832, 2896, 2904]
    gq, gk, gv, ga, gr, dq, dk, dv, iq, ik, iw = [w_in[:, s[i]:s[i + 1]] for i in range(11)]
    zpad = lambda n: jnp.zeros((D_MODEL, n), w_in.dtype)
    w = jnp.concatenate([gq, gk, gv, ga, zpad(LANES - GLA_RANK), gr, dk, ik, zpad(LANES - IDX_DIM)],
                        axis=1).astype(BF16)
    wt = jnp.concatenate([dq, iq, dv, iw, zpad(_T_END - _T_IW - IDX_HEADS)], axis=1).T.astype(BF16)
    wa2p = jnp.concatenate(
        [wa2, jnp.zeros((LANES - GLA_RANK, wa2.shape[1]), wa2.dtype)], axis=0).astype(BF16)
    row = lambda n: pl.BlockSpec((tm, n), lambda i: (i, 0))
    col = lambda n: pl.BlockSpec((n, tm), lambda i: (0, i))
    n_iw = _T_END - _T_IW
    return pl.pallas_call(
        _even_proj_kernel,
        grid=(m // tm,),
        in_specs=[row(D_MODEL), _full((1, D_MODEL)), _full((D_MODEL, _E_END)),
                  _full((_T_END, D_MODEL)), _full((LANES, 256)), _full((1, 256))],
        out_specs=[row(256), row(256), row(512), row(256), row(512), row(DSA_DH), row(IDX_DIM),
                   col(512), col(512),
                   pl.BlockSpec((tm // DSA_TK, DSA_VROWS, DSA_TK), lambda i: (i, 0, 0)),
                   col(n_iw)],
        out_shape=[jax.ShapeDtypeStruct((m, 256), F32), jax.ShapeDtypeStruct((m, 256), F32),
                   jax.ShapeDtypeStruct((m, 512), BF16), jax.ShapeDtypeStruct((m, 256), F32),
                   jax.ShapeDtypeStruct((m, 512), F32), jax.ShapeDtypeStruct((m, DSA_DH), BF16),
                   jax.ShapeDtypeStruct((m, IDX_DIM), BF16),
                   jax.ShapeDtypeStruct((512, m), BF16), jax.ShapeDtypeStruct((512, m), BF16),
                   jax.ShapeDtypeStruct((m // DSA_TK, DSA_VROWS, DSA_TK), BF16),
                   jax.ShapeDtypeStruct((n_iw, m), F32)],
        compiler_params=_cparams("parallel"),
        name="even_proj",
    )(x2, gain.reshape(1, D_MODEL), w, wt, wa2p, ba2.reshape(1, 256))


def _gla_kernel(q_ref, k_ref, v_ref, la_ref, gr_ref, gn_ref, o_ref, st_ref, *, n_chunks):
    c_ = GLA_CHUNK

    @pl.when(pl.program_id(1) == 0)
    def _():
        st_ref[...] = jnp.zeros_like(st_ref)

    ri = lax.broadcasted_iota(jnp.int32, (c_, c_), 0)
    ci = lax.broadcasted_iota(jnp.int32, (c_, c_), 1)
    tri = ri >= ci
    ltri = jnp.where(tri, 1.0, 0.0).astype(BF16)
    gn = gn_ref[...]

    chunks = range(n_chunks)
    heads = range(GLA_HEADS)
    rows = [slice(c * c_, (c + 1) * c_) for c in chunks]
    vs = [slice(h * GLA_DV, (h + 1) * GLA_DV) for h in heads]

    def cumsum(g):
        g1 = g.astype(BF16)
        r1 = g - g1.astype(F32)
        g2 = r1.astype(BF16)
        g3 = (r1 - g2.astype(F32)).astype(BF16)
        return _dot(ltri, g1) + _dot(ltri, g2) + _dot(ltri, g3)

    low_half = lax.broadcasted_iota(jnp.int32, (c_, LANES), 1) < GLA_DK
    pair = [slice((h // 2) * LANES, (h // 2 + 1) * LANES) for h in heads]

    def half(z, h):
        zp = z[:, pair[h]]
        return (jnp.where(low_half, zp, 0.0) if h % 2 == 0 else jnp.where(low_half, 0.0, zp)).astype(BF16)

    cum = [cumsum(la_ref[rows[c], :]) for c in chunks]
    last = [cum[c][c_ - 1:c_, :] for c in chunks]
    q_t = [(q_ref[rows[c], :] * jnp.exp(cum[c]) * (GLA_DK ** -0.5)).astype(BF16) for c in chunks]
    k_t = [k_ref[rows[c], :] * jnp.exp(-cum[c]) for c in chunks]
    k_e = [k_ref[rows[c], :] * jnp.exp(last[c] - cum[c]) for c in chunks]
    dec = [jnp.exp(last[c]) for c in chunks]
    a = [[jnp.where(tri, _dot_nt(q_t[c][:, pair[h]], half(k_t[c], h)), 0.0).astype(BF16)
          for h in heads] for c in chunks]
    o_intra = [[_dot(a[c][h], v_ref[rows[c], vs[h]]) for h in heads] for c in chunks]
    kv = [[_dot_tn(v_ref[rows[c], vs[h]], half(k_e[c], h)) for h in heads] for c in chunks]
    for h in heads:
        st = st_ref[h]
        for c in chunks:
            o = o_intra[c][h] + _dot_nt(q_t[c][:, pair[h]], st.astype(BF16))
            st = dec[c][:, pair[h]] * st + kv[c][h]
            y = _rms(o, gn) * _silu(gr_ref[rows[c], vs[h]])
            o_ref[rows[c], vs[h]] = y.astype(o_ref.dtype)
        st_ref[h] = st


def _gla(gq, gk, gv, la, gr, gnorm, b, t):
    tt = min(GLA_TT, t)
    nt = t // tt
    row = lambda n: pl.BlockSpec((tt, n), lambda i, j: (i * nt + j, 0))
    return pl.pallas_call(
        functools.partial(_gla_kernel, n_chunks=tt // GLA_CHUNK),
        grid=(b, nt),
        in_specs=[row(256), row(256), row(512), row(256), row(512), _full((1, GLA_DV))],
        out_specs=row(512),
        out_shape=jax.ShapeDtypeStruct((b * t, 512), BF16),
        scratch_shapes=[pltpu.VMEM((GLA_HEADS, GLA_DV, LANES), F32)],
        compiler_params=_cparams("parallel", "arbitrary"),
        name="gla",
    )(gq, gk, gv, la, gr, gnorm.reshape(1, GLA_DV))


def _dsa_kernel(iqt_ref, dqt_ref, iwt_ref, ik_ref, dk_ref, dvt_ref, o_ref,
                s_ref, sb_ref, acc_ref, bias_ref, sbuf_ref, *, topk, t_len):
    tq, tk, sk = DSA_TQ, DSA_TK, DSA_SK
    n_sub = tk // sk
    qi = pl.program_id(1)
    q0 = qi * tq
    nkt = (q0 + tq + tk - 1) // tk
    neg_inf = -jnp.inf

    qpos = q0 + lax.broadcasted_iota(jnp.int32, (sk, tq), 1)
    krow = lax.broadcasted_iota(jnp.int32, (sk, tq), 0)
    krow_t = lax.broadcasted_iota(jnp.int32, (tk, tq), 0)

    def colsum(c):
        r = c.shape[0] // (4 * SUBLANES)
        part = jnp.sum(c.reshape(r, 4, SUBLANES, tq), axis=0)
        return jnp.sum(part, axis=0)

    def score_tile(kt, carry, diagonal):
        rmax, rmin = carry
        for j in range(n_sub):
            k0 = pl.multiple_of(kt * tk, tk) + j * sk
            ik_t = ik_ref[pl.ds(k0, sk), :]
            sc = jnp.zeros((sk, tq), F32)
            for h in range(IDX_HEADS):
                lg = _dot(ik_t, iqt_ref[h * IDX_DIM:(h + 1) * IDX_DIM, :])
                sc = sc + iwt_ref[h:h + 1, :] * jnp.maximum(lg, 0.0)
            if diagonal:
                allowed = (k0 + krow) <= qpos
                hi_part, lo_part = jnp.where(allowed, sc, neg_inf), jnp.where(allowed, sc, jnp.inf)
            else:
                hi_part, lo_part = sc, sc
            s_ref[kt, j * sk:(j + 1) * sk, :] = hi_part
            sb_ref[kt, j * sk:(j + 1) * sk, :] = hi_part.astype(BF16)
            rmax = jnp.maximum(rmax, jnp.max(hi_part.reshape(sk // SUBLANES, SUBLANES, tq), axis=0))
            rmin = jnp.minimum(rmin, jnp.min(lo_part.reshape(sk // SUBLANES, SUBLANES, tq), axis=0))
        return rmax, rmin

    def score_pair(i, carry):
        carry = score_tile(2 * i, carry, False)
        return score_tile(jnp.minimum(2 * i + 1, nkt - 2), carry, False)

    carry = lax.fori_loop(0, nkt // 2, score_pair,
                          (jnp.full((SUBLANES, tq), neg_inf, F32), jnp.full((SUBLANES, tq), jnp.inf, F32)))
    rmax, rmin = score_tile(nkt - 1, carry, True)
    rmax = jnp.max(rmax, axis=0, keepdims=True)
    rmin = jnp.min(rmin, axis=0, keepdims=True)

    def over_tiles(body, init):
        carry = lax.fori_loop(0, nkt // 2, lambda i, c: body(2 * i + 1, body(2 * i, c)), init)
        return lax.cond(nkt % 2 == 1, lambda: body(nkt - 1, carry), lambda: carry)

    def count(ind_fn):
        def body(kt, acc):
            return acc + colsum(ind_fn(s_ref[kt], kt))
        acc = over_tiles(body, jnp.zeros((SUBLANES, tq), F32))
        return jnp.sum(acc, axis=0, keepdims=True)

    kf = jnp.float32(topk)

    def count16(mid_b):
        def body(kt, acc):
            ind = jnp.where(sb_ref[kt] >= mid_b, jnp.ones((), BF16), jnp.zeros((), BF16))
            parts = [ind[r * PACKED_ROWS:(r + 1) * PACKED_ROWS, :] for r in range(tk // PACKED_ROWS)]
            while len(parts) > 1:
                parts = [parts[i] + parts[i + 1] for i in range(0, len(parts), 2)]
            return acc + parts[0].astype(F32)
        acc = over_tiles(body, jnp.zeros((PACKED_ROWS, tq), F32))
        return jnp.sum(acc, axis=0, keepdims=True)

    def bisect16(_, carry):
        lo, hi = carry
        mid_b = (lo + (hi - lo) * 0.5).astype(BF16)
        mid = mid_b.astype(F32)
        ge = count16(mid_b) >= kf
        below = mid - jnp.maximum(jnp.abs(mid) * 2.0 ** -7, 1e-30)
        return (jnp.where(ge, jnp.maximum(lo, below), lo), jnp.where(ge, hi, jnp.minimum(hi, mid)))

    def bisect(_, carry):
        lo, hi = carry
        mid = lo + (hi - lo) * 0.5
        ge = count(lambda s, kt: jnp.where(s >= mid, 1.0, 0.0)) >= kf
        return jnp.where(ge, mid, lo), jnp.where(ge, hi, mid)

    lo, hi = lax.fori_loop(0, BISECT16_ITERS, bisect16, (rmin, rmax))
    lo, _ = lax.fori_loop(0, BISECT_ITERS, bisect, (lo, hi))

    def colmin(c):
        return jnp.min(c.reshape(tk // SUBLANES, SUBLANES, tq), axis=0)

    def snap(kt, carry):
        cnt, mn = carry
        s = s_ref[kt]
        ge = s >= lo
        return cnt + colsum(jnp.where(ge, 1.0, 0.0)), jnp.minimum(mn, colmin(jnp.where(ge, s, jnp.inf)))

    c_lo, v0 = over_tiles(snap, (jnp.zeros((SUBLANES, tq), F32), jnp.full((SUBLANES, tq), jnp.inf, F32)))
    c_lo = jnp.sum(c_lo, axis=0, keepdims=True)
    v0 = jnp.min(v0, axis=0, keepdims=True)

    def above(v):
        def body(kt, carry):
            cnt, mn = carry
            s = s_ref[kt]
            gt = s > v
            return (cnt + colsum(jnp.where(gt, 1.0, 0.0)),
                    jnp.minimum(mn, colmin(jnp.where(gt, s, jnp.inf))))
        cnt, mn = over_tiles(body, (jnp.zeros((SUBLANES, tq), F32), jnp.full((SUBLANES, tq), jnp.inf, F32)))
        return jnp.sum(cnt, axis=0, keepdims=True), jnp.min(mn, axis=0, keepdims=True)

    def step_up(carry):
        v, f_v, _, _ = carry
        n_gt, v_next = above(v)
        fin = n_gt < kf
        pending = jnp.max(jnp.where(fin, 0.0, 1.0))
        return jnp.where(fin, v, v_next), jnp.where(fin, f_v, n_gt), n_gt, pending

    thr, f_thr, n_gt, _ = lax.while_loop(
        lambda carry: carry[3] > 0.0, step_up, (v0, c_lo, jnp.zeros((1, tq), F32), jnp.float32(1.0)))
    n_eq = f_thr - n_gt
    need = kf - n_gt

    @pl.when(jnp.max(n_eq - need) > 0.0)
    def _():
        def ibisect(_, carry):
            ilo, ihi = carry
            imid = jnp.right_shift(ilo + ihi, 1)
            c = count(lambda s, kt: jnp.where(
                s == thr, jnp.where((kt * tk + krow_t) <= imid, 1.0, 0.0), 0.0))
            ge = c >= need
            return jnp.where(ge, ilo, imid), jnp.where(ge, imid, ihi)
        ilo0 = jnp.full((1, tq), -1, jnp.int32)
        ihi0 = jnp.full((1, tq), t_len - 1, jnp.int32)
        n_it = max(1, int(t_len - 1).bit_length() + 1)
        jlim = lax.fori_loop(0, n_it, ibisect, (ilo0, ihi0))[1]

        def drop(kt, carry):
            s = s_ref[kt]
            s_ref[kt] = jnp.where(s == thr, jnp.where((kt * tk + krow_t) <= jlim, s, neg_inf), s)
            return carry
        lax.fori_loop(0, nkt, drop, 0)

    acc_ref[...] = jnp.zeros(acc_ref.shape, F32)

    def logits(t, slot):
        tc = jnp.minimum(t, nkt - 1)
        thr_eff = jnp.where(t < nkt, thr, jnp.inf)
        bias_ref[slot] = jnp.where(s_ref[tc] >= thr_eff, 0.0, neg_inf)
        k_t = dk_ref[pl.ds(pl.multiple_of(tc * tk, tk), tk), :]
        for h in range(DSA_HEADS):
            sbuf_ref[slot, h] = _dot(k_t, dqt_ref[h * DSA_DH:(h + 1) * DSA_DH, :]) + bias_ref[slot]

    def softmax_pv(t, slot, ms):
        ms = list(ms)
        vt_t = dvt_ref[jnp.minimum(t, nkt - 1)]
        for h in range(DSA_HEADS):
            m_new = jnp.maximum(ms[h], jnp.max(sbuf_ref[slot, h], axis=0, keepdims=True))
            m_safe = jnp.where(m_new == neg_inf, 0.0, m_new)
            alpha = jnp.exp2(ms[h] - m_safe)
            p = jnp.exp2(sbuf_ref[slot, h] - m_safe).astype(BF16)
            acc_ref[h] = alpha * acc_ref[h] + _dot(vt_t, p)
            ms[h] = m_new
        return tuple(ms)

    def attend(i, ms):
        logits(2 * i + 1, 1)
        ms = softmax_pv(2 * i, 0, ms)
        logits(2 * i + 2, 0)
        return softmax_pv(2 * i + 1, 1, ms)

    logits(0, 0)
    m0 = tuple(jnp.full((1, tq), neg_inf, F32) for _ in range(DSA_HEADS))
    lax.fori_loop(0, (nkt + 1) // 2, attend, m0)
    for h in range(DSA_HEADS):
        num = acc_ref[h, 0:DSA_DH, :]
        den = acc_ref[h, DSA_DH:DSA_DH + 1, :]
        o_ref[:, h * DSA_DH:(h + 1) * DSA_DH] = (num / den).T.astype(o_ref.dtype)


def _dsa(iqt, dqt, iwt, ik, dk, dvt, b, t):
    tq, tk, sk = DSA_TQ, DSA_TK, DSA_SK
    assert tq == tk and t % tq == 0
    nq = t // tq
    topk = min(TOPK_MAX, t // 4)
    qcol = lambda n: pl.BlockSpec((n, tq), lambda i, j: (0, i * nq + j))
    krow = lambda n: pl.BlockSpec((t, n), lambda i, j: (i, 0))
    return pl.pallas_call(
        functools.partial(_dsa_kernel, topk=topk, t_len=t),
        grid=(b, nq),
        in_specs=[qcol(512), qcol(512), qcol(iwt.shape[0]), krow(IDX_DIM), krow(DSA_DH),
                  pl.BlockSpec((t // tk, DSA_VROWS, tk), lambda i, j: (i, 0, 0))],
        out_specs=pl.BlockSpec((tq, 512), lambda i, j: (i * nq + j, 0)),
        out_shape=jax.ShapeDtypeStruct((b * t, 512), BF16),
        scratch_shapes=[
            pltpu.VMEM((t // tk, tk, tq), F32),
            pltpu.VMEM((t // tk, tk, tq), BF16),
            pltpu.VMEM((DSA_HEADS, DSA_VROWS, tq), F32),
            pltpu.VMEM((2, tk, tq), F32),
            pltpu.VMEM((2, DSA_HEADS, tk, tq), F32),
        ],
        compiler_params=_cparams("parallel", "arbitrary"),
        name="dsa",
    )(iqt, dqt, iwt, ik, dk, dvt)


def _mix_ffn_kernel(*refs, n_in, final_norm):
    res_ref = refs[0]
    a_refs = refs[1:1 + n_in]
    w_refs = refs[1 + n_in:1 + 2 * n_in]
    g_ref, wg_ref, wu_ref, wd_ref, fg_ref, o_ref, x_ref = refs[1 + 2 * n_in:]
    x = res_ref[...]
    for a, w in zip(a_refs, w_refs):
        x = x + _dot(a[...], w[...])
    x_ref[...] = x
    xn = _rms(x, g_ref[...]).astype(BF16)
    acc = None
    for a, b in FFN_CHUNKS:
        t = _silu(_dot(xn, wg_ref[:, a:b])) * _dot(xn, wu_ref[:, a:b])
        d = _dot(t.astype(BF16), wd_ref[a:b, :])
        acc = d if acc is None else acc + d
    y = x_ref[...] + acc
    o_ref[...] = _rms(y, fg_ref[...]) if final_norm else y


def _mix_ffn(res, acts, w_outs, gain, wg, wu, wd, final_gain=None):
    m = res.shape[0]
    tm = FFN_TM
    fg = jnp.ones((1, D_MODEL), F32) if final_gain is None else final_gain.reshape(1, D_MODEL)
    row = lambda n: pl.BlockSpec((tm, n), lambda i: (i, 0))
    return pl.pallas_call(
        functools.partial(_mix_ffn_kernel, n_in=len(acts), final_norm=final_gain is not None),
        grid=(m // tm,),
        in_specs=[row(D_MODEL)] + [row(a.shape[1]) for a in acts] + [_resident(w.shape) for w in w_outs]
                 + [_full((1, D_MODEL)), _resident((D_MODEL, D_FF)), _resident((D_MODEL, D_FF)),
                    _resident((D_FF, D_MODEL)), _full((1, D_MODEL))],
        out_specs=row(D_MODEL),
        out_shape=jax.ShapeDtypeStruct((m, D_MODEL), F32),
        scratch_shapes=[pltpu.VMEM((tm, D_MODEL), F32)],
        compiler_params=_cparams("parallel"),
        name="mix_ffn",
    )(res, *acts, *w_outs, gain.reshape(1, D_MODEL), wg.astype(BF16), wu.astype(BF16), wd.astype(BF16), fg)


def _odd_proj_kernel(x_ref, g_ref, w_ref, cos_ref, sin_ref, q_ref, k_ref, v_ref, gate_ref):
    xn = _rms(x_ref[...], g_ref[...]).astype(BF16)
    cos = cos_ref[...]
    sin = sin_ref[...]
    half = RET_DK // 2

    def rot(z, out_ref, scale):
        for h in range(RET_HEADS):
            x1 = z[:, h * RET_DK:h * RET_DK + half]
            x2 = z[:, h * RET_DK + half:(h + 1) * RET_DK]
            out_ref[:, h * RET_DK:h * RET_DK + half] = ((x1 * cos - x2 * sin) * scale).astype(BF16)
            out_ref[:, h * RET_DK + half:(h + 1) * RET_DK] = ((x2 * cos + x1 * sin) * scale).astype(BF16)

    nq = RET_HEADS * RET_DK
    nv = RET_HEADS * RET_DV
    rot(_dot(xn, w_ref[:, 0:nq]), q_ref, 1.0)
    rot(_dot(xn, w_ref[:, nq:2 * nq]), k_ref, RET_DK ** -0.5)
    v_ref[...] = _dot(xn, w_ref[:, 2 * nq:2 * nq + nv]).astype(BF16)
    gate_ref[...] = _dot(xn, w_ref[:, 2 * nq + nv:2 * nq + 2 * nv])


def _odd_proj(h, gain, w_in, t):
    m = h.shape[0]
    tm = PROJ_TM
    nt = t // tm
    half = RET_DK // 2
    inv = ROPE_BASE ** (-jnp.arange(half, dtype=F32) / half)
    ang = jnp.arange(t, dtype=jnp.int32).astype(F32)[:, None] * inv[None, :]
    row = lambda n: pl.BlockSpec((tm, n), lambda i: (i, 0))
    pos = pl.BlockSpec((tm, half), lambda i: (i % nt, 0))
    nq = RET_HEADS * RET_DK
    nv = RET_HEADS * RET_DV
    return pl.pallas_call(
        _odd_proj_kernel,
        grid=(m // tm,),
        in_specs=[row(D_MODEL), _full((1, D_MODEL)), _full(w_in.shape), pos, pos],
        out_specs=[row(nq), row(nq), row(nv), row(nv)],
        out_shape=[jax.ShapeDtypeStruct((m, nq), BF16), jax.ShapeDtypeStruct((m, nq), BF16),
                   jax.ShapeDtypeStruct((m, nv), BF16), jax.ShapeDtypeStruct((m, nv), F32)],
        compiler_params=_cparams("parallel"),
        name="odd_proj",
    )(h, gain.reshape(1, D_MODEL), w_in.astype(BF16), jnp.cos(ang), jnp.sin(ang))


def _ret_kernel(dec_ref, q_ref, k_ref, v_ref, gate_ref, dm_ref, xi_ref, ze_ref, rn_ref,
                o_ref, st_ref, *, n_chunks):
    c_ = RET_CHUNK

    @pl.when(pl.program_id(1) == 0)
    def _():
        st_ref[...] = jnp.zeros_like(st_ref)

    rn = rn_ref[...]
    for c in range(n_chunks):
        rows = slice(c * c_, (c + 1) * c_)
        for h in range(RET_HEADS):
            ks = slice(h * RET_DK, (h + 1) * RET_DK)
            vs = slice(h * RET_DV, (h + 1) * RET_DV)
            q = q_ref[rows, ks]
            k = k_ref[rows, ks]
            v = v_ref[rows, vs]
            st = st_ref[h]
            inner = (_dot_nt(q, k) * dm_ref[h]).astype(BF16)
            o = _dot(inner, v) + xi_ref[h] * _dot(q, st.astype(BF16))
            kz = (k.astype(F32) * ze_ref[h]).astype(BF16)
            st_ref[h] = dec_ref[h] * st + _dot_tn(kz, v)
            y = _rms(o, rn) * _silu(gate_ref[rows, vs])
            o_ref[rows, vs] = y.astype(o_ref.dtype)


def _retention(q, k, v, gate, rnorm, b, t):
    c_ = min(RET_CHUNK, t)
    tt = min(RET_TT, t)
    nt = t // tt
    log_g = jnp.log1p(-jnp.exp2(-5.0 - jnp.arange(RET_HEADS, dtype=F32)))
    idx = jnp.arange(c_, dtype=F32)
    rel = idx[:, None] - idx[None, :]
    dmat = jnp.where(rel >= 0, jnp.exp(log_g[:, None, None] * jnp.maximum(rel, 0.0)), 0.0)
    xi = jnp.exp(log_g[:, None] * (idx[None, :] + 1.0))[:, :, None]
    zeta = jnp.exp(log_g[:, None] * (c_ - 1.0 - idx[None, :]))[:, :, None]
    decay_c = jnp.exp(log_g * c_)
    nq = RET_HEADS * RET_DK
    nv = RET_HEADS * RET_DV
    row = lambda n: pl.BlockSpec((tt, n), lambda i, j: (i * nt + j, 0))
    return pl.pallas_call(
        functools.partial(_ret_kernel, n_chunks=tt // c_),
        grid=(b, nt),
        in_specs=[pl.BlockSpec(memory_space=pltpu.SMEM),
                  row(nq), row(nq), row(nv), row(nv),
                  _full((RET_HEADS, c_, c_)), _full((RET_HEADS, c_, 1)), _full((RET_HEADS, c_, 1)),
                  _full((1, RET_DV))],
        out_specs=row(nv),
        out_shape=jax.ShapeDtypeStruct((b * t, nv), BF16),
        scratch_shapes=[pltpu.VMEM((RET_HEADS, RET_DK, RET_DV), F32)],
        compiler_params=_cparams("parallel", "arbitrary"),
        name="retention",
    )(decay_c, q, k, v, gate, dmat, xi, zeta, rnorm.reshape(1, RET_DV))


def kernel(x, even_attn_norm, even_w_in, even_gla_wa2, even_gla_ba2, even_gla_norm, even_w_out,
           odd_attn_norm, odd_w_in, odd_ret_norm, odd_w_out,
           ffn_norm, ffn_w_gate, ffn_w_up, ffn_w_down, final_norm):
    b, t, d = x.shape
    h = x.reshape(b * t, d)

    gq, gk, gv, la, gr, dk, ik, dqt, iqt, dvt, iwt = _even_proj(
        h, even_attn_norm[0], even_w_in[0], even_gla_wa2[0], even_gla_ba2[0])
    o_gla = _gla(gq, gk, gv, la, gr, even_gla_norm[0], b, t)
    o_dsa = _dsa(iqt, dqt, iwt, ik, dk, dvt, b, t)
    w_out = even_w_out[0].astype(BF16)
    n_gla = GLA_HEADS * GLA_DV
    h = _mix_ffn(h, [o_gla, o_dsa], [w_out[:n_gla], w_out[n_gla:]],
                 ffn_norm[0], ffn_w_gate[0], ffn_w_up[0], ffn_w_down[0])

    q, k, v, gate = _odd_proj(h, odd_attn_norm[0], odd_w_in[0], t)
    o_ret = _retention(q, k, v, gate, odd_ret_norm[0], b, t)
    h = _mix_ffn(h, [o_ret], [odd_w_out[0].astype(BF16)],
                 ffn_norm[1], ffn_w_gate[1], ffn_w_up[1], ffn_w_down[1], final_gain=final_norm)
    return h.reshape(b, t, d)
```

```python
import functools

import jax
import jax.numpy as jnp
from jax import lax
from jax.experimental import pallas as pl
from jax.experimental.pallas import tpu as pltpu

F32 = jnp.float32
BF16 = jnp.bfloat16

D_MODEL = 1024
EPS = 1e-6
GLA_HEADS = 4
GLA_DV = 128
GLA_DK = 64
GLA_RANK = 16
GLA_GATE_NORM = 16.0
GLA_CHUNK = 64
DSA_HEADS = 4
DSA_DH = 128
IDX_HEADS = 8
IDX_DIM = 64
TOPK_MAX = 256
RET_HEADS = 4
RET_DK = 256
RET_DV = 512
ROPE_BASE = 10000.0
D_FF = 2816

LANES = 128
SUBLANES = 8
VMEM_LIMIT = 56 * 1024 * 1024

PROJ_TM = 512
FFN_TM = 512
FFN_CHUNKS = ((0, 1024), (1024, 2048), (2048, 2816))
GLA_TT = 512
RET_CHUNK = 256
RET_TT = 1024
DSA_TQ = 256
DSA_TK = 256
DSA_SK = 128
DSA_VROWS = DSA_DH + 16
LOG2E = 1.4426950408889634
PACKED_ROWS = 16
BISECT16_ITERS = 8
BISECT_ITERS = 8


def _cparams(*sem):
    return pltpu.CompilerParams(dimension_semantics=sem, vmem_limit_bytes=VMEM_LIMIT)


def _dot(a, b):
    return jnp.dot(a, b, preferred_element_type=F32)


def _dot_nt(a, b):
    return lax.dot_general(a, b, (((1,), (1,)), ((), ())), preferred_element_type=F32)


def _dot_tn(a, b):
    return lax.dot_general(a, b, (((0,), (0,)), ((), ())), preferred_element_type=F32)


def _rms(x, g):
    return x * lax.rsqrt(jnp.mean(x * x, axis=-1, keepdims=True) + EPS) * g


def _silu(x):
    return x * jax.nn.sigmoid(x)


def _full(shape):
    return pl.BlockSpec(shape, lambda *_: (0,) * len(shape))


def _resident(shape):
    return pl.BlockSpec(shape, lambda *_: (0,) * len(shape), pipeline_mode=pl.Buffered(1))


_E_GQ, _E_GK, _E_GV, _E_GA, _E_GR, _E_DK, _E_IK, _E_END = (0, 256, 512, 1024, 1152, 1664, 1792, 1920)
_T_DQ, _T_IQ, _T_DV, _T_IW, _T_END = (0, 512, 1024, 1152, 1168)


def _even_proj_kernel(x_ref, g_ref, w_ref, wt_ref, wa2_ref, ba2_ref,
                      gq_ref, gk_ref, gv_ref, la_ref, gr_ref, dk_ref, ik_ref,
                      dqt_ref, iqt_ref, dvt_ref, iwt_ref):
    xn = _rms(x_ref[...], g_ref[...]).astype(BF16)

    def proj(a, b):
        return _dot(xn, w_ref[:, a:b])

    def proj_t(a, b):
        return _dot_nt(wt_ref[a:b, :], xn)

    ga = proj(_E_GA, _E_GR).astype(BF16)
    gq_ref[...] = proj(_E_GQ, _E_GK)
    gk_ref[...] = proj(_E_GK, _E_GV)
    gv_ref[...] = proj(_E_GV, _E_GA).astype(BF16)
    gr_ref[...] = proj(_E_GR, _E_DK)
    dk_ref[...] = proj(_E_DK, _E_IK).astype(BF16)
    ik_ref[...] = proj(_E_IK, _E_IK + IDX_DIM).astype(BF16)
    dqt_ref[...] = (proj_t(_T_DQ, _T_IQ) * (DSA_DH ** -0.5 * LOG2E)).astype(BF16)
    iqt_ref[...] = proj_t(_T_IQ, _T_DV).astype(BF16)
    dvt = proj_t(_T_DV, _T_IW).astype(BF16)
    for c in range(dvt_ref.shape[0]):
        dvt_ref[c, 0:DSA_DH, :] = dvt[:, c * DSA_TK:(c + 1) * DSA_TK]
        dvt_ref[c, DSA_DH:DSA_VROWS, :] = jnp.ones((DSA_VROWS - DSA_DH, DSA_TK), BF16)
    iwt_ref[...] = proj_t(_T_IW, _T_END) * ((IDX_HEADS ** -0.5) * (IDX_DIM ** -0.5))
    z = _dot(ga, wa2_ref[...]) + ba2_ref[...]
    la_ref[...] = jax.nn.log_sigmoid(z) * (1.0 / GLA_GATE_NORM)


def _even_proj(x2, gain, w_in, wa2, ba2):
    m = x2.shape[0]
    tm = PROJ_TM
    s = [0, 256, 512, 1024, 1040, 1552, 2064, 2192, 2320, 2832, 2896, 2904]
    gq, gk, gv, ga, gr, dq, dk, dv, iq, ik, iw = [w_in[:, s[i]:s[i + 1]] for i in range(11)]
    zpad = lambda n: jnp.zeros((D_MODEL, n), w_in.dtype)
    w = jnp.concatenate([gq, gk, gv, ga, zpad(LANES - GLA_RANK), gr, dk, ik, zpad(LANES - IDX_DIM)],
                        axis=1).astype(BF16)
    wt = jnp.concatenate([dq, iq, dv, iw, zpad(_T_END - _T_IW - IDX_HEADS)], axis=1).T.astype(BF16)
    wa2p = jnp.concatenate(
        [wa2, jnp.zeros((LANES - GLA_RANK, wa2.shape[1]), wa2.dtype)], axis=0).astype(BF16)
    row = lambda n: pl.BlockSpec((tm, n), lambda i: (i, 0))
    col = lambda n: pl.BlockSpec((n, tm), lambda i: (0, i))
    n_iw = _T_END - _T_IW
    return pl.pallas_call(
        _even_proj_kernel,
        grid=(m // tm,),
        in_specs=[row(D_MODEL), _full((1, D_MODEL)), _full((D_MODEL, _E_END)),
                  _full((_T_END, D_MODEL)), _full((LANES, 256)), _full((1, 256))],
        out_specs=[row(256), row(256), row(512), row(256), row(512), row(DSA_DH), row(IDX_DIM),
                   col(512), col(512),
                   pl.BlockSpec((tm // DSA_TK, DSA_VROWS, DSA_TK), lambda i: (i, 0, 0)),
                   col(n_iw)],
        out_shape=[jax.ShapeDtypeStruct((m, 256), F32), jax.ShapeDtypeStruct((m, 256), F32),
                   jax.ShapeDtypeStruct((m, 512), BF16), jax.ShapeDtypeStruct((m, 256), F32),
                   jax.ShapeDtypeStruct((m, 512), F32), jax.ShapeDtypeStruct((m, DSA_DH), BF16),
                   jax.ShapeDtypeStruct((m, IDX_DIM), BF16),
                   jax.ShapeDtypeStruct((512, m), BF16), jax.ShapeDtypeStruct((512, m), BF16),
                   jax.ShapeDtypeStruct((m // DSA_TK, DSA_VROWS, DSA_TK), BF16),
                   jax.ShapeDtypeStruct((n_iw, m), F32)],
        compiler_params=_cparams("parallel"),
        name="even_proj",
    )(x2, gain.reshape(1, D_MODEL), w, wt, wa2p, ba2.reshape(1, 256))


def _gla_kernel(q_ref, k_ref, v_ref, la_ref, gr_ref, gn_ref, o_ref, st_ref, *, n_chunks):
    c_ = GLA_CHUNK

    @pl.when(pl.program_id(1) == 0)
    def _():
        st_ref[...] = jnp.zeros_like(st_ref)

    ri = lax.broadcasted_iota(jnp.int32, (c_, c_), 0)
    ci = lax.broadcasted_iota(jnp.int32, (c_, c_), 1)
    tri = ri >= ci
    ltri = jnp.where(tri, 1.0, 0.0).astype(BF16)
    gn = gn_ref[...]

    chunks = range(n_chunks)
    heads = range(GLA_HEADS)
    rows = [slice(c * c_, (c + 1) * c_) for c in chunks]
    vs = [slice(h * GLA_DV, (h + 1) * GLA_DV) for h in heads]

    def cumsum(g):
        g1 = g.astype(BF16)
        r1 = g - g1.astype(F32)
        g2 = r1.astype(BF16)
        g3 = (r1 - g2.astype(F32)).astype(BF16)
        return _dot(ltri, g1) + _dot(ltri, g2) + _dot(ltri, g3)

    low_half = lax.broadcasted_iota(jnp.int32, (c_, LANES), 1) < GLA_DK
    pair = [slice((h // 2) * LANES, (h // 2 + 1) * LANES) for h in heads]

    def half(z, h):
        zp = z[:, pair[h]]
        return (jnp.where(low_half, zp, 0.0) if h % 2 == 0 else jnp.where(low_half, 0.0, zp)).astype(BF16)

    cum = [cumsum(la_ref[rows[c], :]) for c in chunks]
    last = [cum[c][c_ - 1:c_, :] for c in chunks]
    q_t = [(q_ref[rows[c], :] * jnp.exp(cum[c]) * (GLA_DK ** -0.5)).astype(BF16) for c in chunks]
    k_t = [k_ref[rows[c], :] * jnp.exp(-cum[c]) for c in chunks]
    k_e = [k_ref[rows[c], :] * jnp.exp(last[c] - cum[c]) for c in chunks]
    dec = [jnp.exp(last[c]) for c in chunks]
    a = [[jnp.where(tri, _dot_nt(q_t[c][:, pair[h]], half(k_t[c], h)), 0.0).astype(BF16)
          for h in heads] for c in chunks]
    o_intra = [[_dot(a[c][h], v_ref[rows[c], vs[h]]) for h in heads] for c in chunks]
    kv = [[_dot_tn(v_ref[rows[c], vs[h]], half(k_e[c], h)) for h in heads] for c in chunks]
    for h in heads:
        st = st_ref[h]
        for c in chunks:
            o = o_intra[c][h] + _dot_nt(q_t[c][:, pair[h]], st.astype(BF16))
            st = dec[c][:, pair[h]] * st + kv[c][h]
            y = _rms(o, gn) * _silu(gr_ref[rows[c], vs[h]])
            o_ref[rows[c], vs[h]] = y.astype(o_ref.dtype)
        st_ref[h] = st


def _gla(gq, gk, gv, la, gr, gnorm, b, t):
    tt = min(GLA_TT, t)
    nt = t // tt
    row = lambda n: pl.BlockSpec((tt, n), lambda i, j: (i * nt + j, 0))
    return pl.pallas_call(
        functools.partial(_gla_kernel, n_chunks=tt // GLA_CHUNK),
        grid=(b, nt),
        in_specs=[row(256), row(256), row(512), row(256), row(512), _full((1, GLA_DV))],
        out_specs=row(512),
        out_shape=jax.ShapeDtypeStruct((b * t, 512), BF16),
        scratch_shapes=[pltpu.VMEM((GLA_HEADS, GLA_DV, LANES), F32)],
        compiler_params=_cparams("parallel", "arbitrary"),
        name="gla",
    )(gq, gk, gv, la, gr, gnorm.reshape(1, GLA_DV))


def _dsa_kernel(iqt_ref, dqt_ref, iwt_ref, ik_ref, dk_ref, dvt_ref, o_ref,
                s_ref, sb_ref, acc_ref, bias_ref, sbuf_ref, *, topk, t_len):
    tq, tk, sk = DSA_TQ, DSA_TK, DSA_SK
    n_sub = tk // sk
    qi = pl.program_id(1)
    q0 = qi * tq
    nkt = (q0 + tq + tk - 1) // tk
    neg_inf = -jnp.inf

    qpos = q0 + lax.broadcasted_iota(jnp.int32, (sk, tq), 1)
    krow = lax.broadcasted_iota(jnp.int32, (sk, tq), 0)
    krow_t = lax.broadcasted_iota(jnp.int32, (tk, tq), 0)

    def colsum(c):
        r = c.shape[0] // (4 * SUBLANES)
        part = jnp.sum(c.reshape(r, 4, SUBLANES, tq), axis=0)
        return jnp.sum(part, axis=0)

    def score_tile(kt, carry, diagonal):
        rmax, rmin = carry
        for j in range(n_sub):
            k0 = pl.multiple_of(kt * tk, tk) + j * sk
            ik_t = ik_ref[pl.ds(k0, sk), :]
            sc = jnp.zeros((sk, tq), F32)
            for h in range(IDX_HEADS):
                lg = _dot(ik_t, iqt_ref[h * IDX_DIM:(h + 1) * IDX_DIM, :])
                sc = sc + iwt_ref[h:h + 1, :] * jnp.maximum(lg, 0.0)
            if diagonal:
                allowed = (k0 + krow) <= qpos
                hi_part, lo_part = jnp.where(allowed, sc, neg_inf), jnp.where(allowed, sc, jnp.inf)
            else:
                hi_part, lo_part = sc, sc
            s_ref[kt, j * sk:(j + 1) * sk, :] = hi_part
            sb_ref[kt, j * sk:(j + 1) * sk, :] = hi_part.astype(BF16)
            rmax = jnp.maximum(rmax, jnp.max(hi_part.reshape(sk // SUBLANES, SUBLANES, tq), axis=0))
            rmin = jnp.minimum(rmin, jnp.min(lo_part.reshape(sk // SUBLANES, SUBLANES, tq), axis=0))
        return rmax, rmin

    def score_pair(i, carry):
        carry = score_tile(2 * i, carry, False)
        return score_tile(jnp.minimum(2 * i + 1, nkt - 2), carry, False)

    carry = lax.fori_loop(0, nkt // 2, score_pair,
                          (jnp.full((SUBLANES, tq), neg_inf, F32), jnp.full((SUBLANES, tq), jnp.inf, F32)))
    rmax, rmin = score_tile(nkt - 1, carry, True)
    rmax = jnp.max(rmax, axis=0, keepdims=True)
    rmin = jnp.min(rmin, axis=0, keepdims=True)

    def over_tiles(body, init):
        carry = lax.fori_loop(0, nkt // 2, lambda i, c: body(2 * i + 1, body(2 * i, c)), init)
        return lax.cond(nkt % 2 == 1, lambda: body(nkt - 1, carry), lambda: carry)

    def count(ind_fn):
        def body(kt, acc):
            return acc + colsum(ind_fn(s_ref[kt], kt))
        acc = over_tiles(body, jnp.zeros((SUBLANES, tq), F32))
        return jnp.sum(acc, axis=0, keepdims=True)

    kf = jnp.float32(topk)

    def count16(mid_b):
        def body(kt, acc):
            ind = jnp.where(sb_ref[kt] >= mid_b, jnp.ones((), BF16), jnp.zeros((), BF16))
            parts = [ind[r * PACKED_ROWS:(r + 1) * PACKED_ROWS, :] for r in range(tk // PACKED_ROWS)]
            while len(parts) > 1:
                parts = [parts[i] + parts[i + 1] for i in range(0, len(parts), 2)]
            return acc + parts[0].astype(F32)
        acc = over_tiles(body, jnp.zeros((PACKED_ROWS, tq), F32))
        return jnp.sum(acc, axis=0, keepdims=True)

    def bisect16(_, carry):
        lo, hi = carry
        mid_b = (lo + (hi - lo) * 0.5).astype(BF16)
        mid = mid_b.astype(F32)
        ge = count16(mid_b) >= kf
        below = mid - jnp.maximum(jnp.abs(mid) * 2.0 ** -7, 1e-30)
        return (jnp.where(ge, jnp.maximum(lo, below), lo), jnp.where(ge, hi, jnp.minimum(hi, mid)))

    def bisect(_, carry):
        lo, hi = carry
        mid = lo + (hi - lo) * 0.5
        ge = count(lambda s, kt: jnp.where(s >= mid, 1.0, 0.0)) >= kf
        return jnp.where(ge, mid, lo), jnp.where(ge, hi, mid)

    lo, hi = lax.fori_loop(0, BISECT16_ITERS, bisect16, (rmin, rmax))
    lo, _ = lax.fori_loop(0, BISECT_ITERS, bisect, (lo, hi))

    def colmin(c):
        return jnp.min(c.reshape(tk // SUBLANES, SUBLANES, tq), axis=0)

    def snap(kt, carry):
        cnt, mn = carry
        s = s_ref[kt]
        ge = s >= lo
        return cnt + colsum(jnp.where(ge, 1.0, 0.0)), jnp.minimum(mn, colmin(jnp.where(ge, s, jnp.inf)))

    c_lo, v0 = over_tiles(snap, (jnp.zeros((SUBLANES, tq), F32), jnp.full((SUBLANES, tq), jnp.inf, F32)))
    c_lo = jnp.sum(c_lo, axis=0, keepdims=True)
    v0 = jnp.min(v0, axis=0, keepdims=True)

    def above(v):
        def body(kt, carry):
            cnt, mn = carry
            s = s_ref[kt]
            gt = s > v
            return (cnt + colsum(jnp.where(gt, 1.0, 0.0)),
                    jnp.minimum(mn, colmin(jnp.where(gt, s, jnp.inf))))
        cnt, mn = over_tiles(body, (jnp.zeros((SUBLANES, tq), F32), jnp.full((SUBLANES, tq), jnp.inf, F32)))
        return jnp.sum(cnt, axis=0, keepdims=True), jnp.min(mn, axis=0, keepdims=True)

    def step_up(carry):
        v, f_v, _, _ = carry
        n_gt, v_next = above(v)
        fin = n_gt < kf
        pending = jnp.max(jnp.where(fin, 0.0, 1.0))
        return jnp.where(fin, v, v_next), jnp.where(fin, f_v, n_gt), n_gt, pending

    thr, f_thr, n_gt, _ = lax.while_loop(
        lambda carry: carry[3] > 0.0, step_up, (v0, c_lo, jnp.zeros((1, tq), F32), jnp.float32(1.0)))
    n_eq = f_thr - n_gt
    need = kf - n_gt

    @pl.when(jnp.max(n_eq - need) > 0.0)
    def _():
        def ibisect(_, carry):
            ilo, ihi = carry
            imid = jnp.right_shift(ilo + ihi, 1)
            c = count(lambda s, kt: jnp.where(
                s == thr, jnp.where((kt * tk + krow_t) <= imid, 1.0, 0.0), 0.0))
            ge = c >= need
            return jnp.where(ge, ilo, imid), jnp.where(ge, imid, ihi)
        ilo0 = jnp.full((1, tq), -1, jnp.int32)
        ihi0 = jnp.full((1, tq), t_len - 1, jnp.int32)
        n_it = max(1, int(t_len - 1).bit_length() + 1)
        jlim = lax.fori_loop(0, n_it, ibisect, (ilo0, ihi0))[1]

        def drop(kt, carry):
            s = s_ref[kt]
            s_ref[kt] = jnp.where(s == thr, jnp.where((kt * tk + krow_t) <= jlim, s, neg_inf), s)
            return carry
        lax.fori_loop(0, nkt, drop, 0)

    acc_ref[...] = jnp.zeros(acc_ref.shape, F32)

    def logits(t, slot):
        tc = jnp.minimum(t, nkt - 1)
        thr_eff = jnp.where(t < nkt, thr, jnp.inf)
        bias_ref[slot] = jnp.where(s_ref[tc] >= thr_eff, 0.0, neg_inf)
        k_t = dk_ref[pl.ds(pl.multiple_of(tc * tk, tk), tk), :]
        for h in range(DSA_HEADS):
            sbuf_ref[slot, h] = _dot(k_t, dqt_ref[h * DSA_DH:(h + 1) * DSA_DH, :]) + bias_ref[slot]

    def softmax_pv(t, slot, ms):
        ms = list(ms)
        vt_t = dvt_ref[jnp.minimum(t, nkt - 1)]
        for h in range(DSA_HEADS):
            m_new = jnp.maximum(ms[h], jnp.max(sbuf_ref[slot, h], axis=0, keepdims=True))
            m_safe = jnp.where(m_new == neg_inf, 0.0, m_new)
            alpha = jnp.exp2(ms[h] - m_safe)
            p = jnp.exp2(sbuf_ref[slot, h] - m_safe).astype(BF16)
            acc_ref[h] = alpha * acc_ref[h] + _dot(vt_t, p)
            ms[h] = m_new
        return tuple(ms)

    def attend(i, ms):
        logits(2 * i + 1, 1)
        ms = softmax_pv(2 * i, 0, ms)
        logits(2 * i + 2, 0)
        return softmax_pv(2 * i + 1, 1, ms)

    logits(0, 0)
    m0 = tuple(jnp.full((1, tq), neg_inf, F32) for _ in range(DSA_HEADS))
    lax.fori_loop(0, (nkt + 1) // 2, attend, m0)
    for h in range(DSA_HEADS):
        num = acc_ref[h, 0:DSA_DH, :]
        den = acc_ref[h, DSA_DH:DSA_DH + 1, :]
        o_ref[:, h * DSA_DH:(h + 1) * DSA_DH] = (num / den).T.astype(o_ref.dtype)


def _dsa(iqt, dqt, iwt, ik, dk, dvt, b, t):
    tq, tk, sk = DSA_TQ, DSA_TK, DSA_SK
    assert tq == tk and t % tq == 0
    nq = t // tq
    topk = min(TOPK_MAX, t // 4)
    qcol = lambda n: pl.BlockSpec((n, tq), lambda i, j: (0, i * nq + j))
    krow = lambda n: pl.BlockSpec((t, n), lambda i, j: (i, 0))
    return pl.pallas_call(
        functools.partial(_dsa_kernel, topk=topk, t_len=t),
        grid=(b, nq),
        in_specs=[qcol(512), qcol(512), qcol(iwt.shape[0]), krow(IDX_DIM), krow(DSA_DH),
                  pl.BlockSpec((t // tk, DSA_VROWS, tk), lambda i, j: (i, 0, 0))],
        out_specs=pl.BlockSpec((tq, 512), lambda i, j: (i * nq + j, 0)),
        out_shape=jax.ShapeDtypeStruct((b * t, 512), BF16),
        scratch_shapes=[
            pltpu.VMEM((t // tk, tk, tq), F32),
            pltpu.VMEM((t // tk, tk, tq), BF16),
            pltpu.VMEM((DSA_HEADS, DSA_VROWS, tq), F32),
            pltpu.VMEM((2, tk, tq), F32),
            pltpu.VMEM((2, DSA_HEADS, tk, tq), F32),
        ],
        compiler_params=_cparams("parallel", "arbitrary"),
        name="dsa",
    )(iqt, dqt, iwt, ik, dk, dvt)


def _mix_ffn_kernel(*refs, n_in, final_norm):
    res_ref = refs[0]
    a_refs = refs[1:1 + n_in]
    w_refs = refs[1 + n_in:1 + 2 * n_in]
    g_ref, wg_ref, wu_ref, wd_ref, fg_ref, o_ref, x_ref = refs[1 + 2 * n_in:]
    x = res_ref[...]
    for a, w in zip(a_refs, w_refs):
        x = x + _dot(a[...], w[...])
    x_ref[...] = x
    xn = _rms(x, g_ref[...]).astype(BF16)
    acc = None
    for a, b in FFN_CHUNKS:
        t = _silu(_dot(xn, wg_ref[:, a:b])) * _dot(xn, wu_ref[:, a:b])
        d = _dot(t.astype(BF16), wd_ref[a:b, :])
        acc = d if acc is None else acc + d
    y = x_ref[...] + acc
    o_ref[...] = _rms(y, fg_ref[...]) if final_norm else y


def _mix_ffn(res, acts, w_outs, gain, wg, wu, wd, final_gain=None):
    m = res.shape[0]
    tm = FFN_TM
    fg = jnp.ones((1, D_MODEL), F32) if final_gain is None else final_gain.reshape(1, D_MODEL)
    row = lambda n: pl.BlockSpec((tm, n), lambda i: (i, 0))
    return pl.pallas_call(
        functools.partial(_mix_ffn_kernel, n_in=len(acts), final_norm=final_gain is not None),
        grid=(m // tm,),
        in_specs=[row(D_MODEL)] + [row(a.shape[1]) for a in acts] + [_resident(w.shape) for w in w_outs]
                 + [_full((1, D_MODEL)), _resident((D_MODEL, D_FF)), _resident((D_MODEL, D_FF)),
                    _resident((D_FF, D_MODEL)), _full((1, D_MODEL))],
        out_specs=row(D_MODEL),
        out_shape=jax.ShapeDtypeStruct((m, D_MODEL), F32),
        scratch_shapes=[pltpu.VMEM((tm, D_MODEL), F32)],
        compiler_params=_cparams("parallel"),
        name="mix_ffn",
    )(res, *acts, *w_outs, gain.reshape(1, D_MODEL), wg.astype(BF16), wu.astype(BF16), wd.astype(BF16), fg)


def _odd_proj_kernel(x_ref, g_ref, w_ref, cos_ref, sin_ref, q_ref, k_ref, v_ref, gate_ref):
    xn = _rms(x_ref[...], g_ref[...]).astype(BF16)
    cos = cos_ref[...]
    sin = sin_ref[...]
    half = RET_DK // 2

    def rot(z, out_ref, scale):
        for h in range(RET_HEADS):
            x1 = z[:, h * RET_DK:h * RET_DK + half]
            x2 = z[:, h * RET_DK + half:(h + 1) * RET_DK]
            out_ref[:, h * RET_DK:h * RET_DK + half] = ((x1 * cos - x2 * sin) * scale).astype(BF16)
            out_ref[:, h * RET_DK + half:(h + 1) * RET_DK] = ((x2 * cos + x1 * sin) * scale).astype(BF16)

    nq = RET_HEADS * RET_DK
    nv = RET_HEADS * RET_DV
    rot(_dot(xn, w_ref[:, 0:nq]), q_ref, 1.0)
    rot(_dot(xn, w_ref[:, nq:2 * nq]), k_ref, RET_DK ** -0.5)
    v_ref[...] = _dot(xn, w_ref[:, 2 * nq:2 * nq + nv]).astype(BF16)
    gate_ref[...] = _dot(xn, w_ref[:, 2 * nq + nv:2 * nq + 2 * nv])


def _odd_proj(h, gain, w_in, t):
    m = h.shape[0]
    tm = PROJ_TM
    nt = t // tm
    half = RET_DK // 2
    inv = ROPE_BASE ** (-jnp.arange(half, dtype=F32) / half)
    ang = jnp.arange(t, dtype=jnp.int32).astype(F32)[:, None] * inv[None, :]
    row = lambda n: pl.BlockSpec((tm, n), lambda i: (i, 0))
    pos = pl.BlockSpec((tm, half), lambda i: (i % nt, 0))
    nq = RET_HEADS * RET_DK
    nv = RET_HEADS * RET_DV
    return pl.pallas_call(
        _odd_proj_kernel,
        grid=(m // tm,),
        in_specs=[row(D_MODEL), _full((1, D_MODEL)), _full(w_in.shape), pos, pos],
        out_specs=[row(nq), row(nq), row(nv), row(nv)],
        out_shape=[jax.ShapeDtypeStruct((m, nq), BF16), jax.ShapeDtypeStruct((m, nq), BF16),
                   jax.ShapeDtypeStruct((m, nv), BF16), jax.ShapeDtypeStruct((m, nv), F32)],
        compiler_params=_cparams("parallel"),
        name="odd_proj",
    )(h, gain.reshape(1, D_MODEL), w_in.astype(BF16), jnp.cos(ang), jnp.sin(ang))


def _ret_kernel(dec_ref, q_ref, k_ref, v_ref, gate_ref, dm_ref, xi_ref, ze_ref, rn_ref,
                o_ref, st_ref, *, n_chunks):
    c_ = RET_CHUNK

    @pl.when(pl.program_id(1) == 0)
    def _():
        st_ref[...] = jnp.zeros_like(st_ref)

    rn = rn_ref[...]
    for c in range(n_chunks):
        rows = slice(c * c_, (c + 1) * c_)
        for h in range(RET_HEADS):
            ks = slice(h * RET_DK, (h + 1) * RET_DK)
            vs = slice(h * RET_DV, (h + 1) * RET_DV)
            q = q_ref[rows, ks]
            k = k_ref[rows, ks]
            v = v_ref[rows, vs]
            st = st_ref[h]
            inner = (_dot_nt(q, k) * dm_ref[h]).astype(BF16)
            o = _dot(inner, v) + xi_ref[h] * _dot(q, st.astype(BF16))
            kz = (k.astype(F32) * ze_ref[h]).astype(BF16)
            st_ref[h] = dec_ref[h] * st + _dot_tn(kz, v)
            y = _rms(o, rn) * _silu(gate_ref[rows, vs])
            o_ref[rows, vs] = y.astype(o_ref.dtype)


def _retention(q, k, v, gate, rnorm, b, t):
    c_ = min(RET_CHUNK, t)
    tt = min(RET_TT, t)
    nt = t // tt
    log_g = jnp.log1p(-jnp.exp2(-5.0 - jnp.arange(RET_HEADS, dtype=F32)))
    idx = jnp.arange(c_, dtype=F32)
    rel = idx[:, None] - idx[None, :]
    dmat = jnp.where(rel >= 0, jnp.exp(log_g[:, None, None] * jnp.maximum(rel, 0.0)), 0.0)
    xi = jnp.exp(log_g[:, None] * (idx[None, :] + 1.0))[:, :, None]
    zeta = jnp.exp(log_g[:, None] * (c_ - 1.0 - idx[None, :]))[:, :, None]
    decay_c = jnp.exp(log_g * c_)
    nq = RET_HEADS * RET_DK
    nv = RET_HEADS * RET_DV
    row = lambda n: pl.BlockSpec((tt, n), lambda i, j: (i * nt + j, 0))
    return pl.pallas_call(
        functools.partial(_ret_kernel, n_chunks=tt // c_),
        grid=(b, nt),
        in_specs=[pl.BlockSpec(memory_space=pltpu.SMEM),
                  row(nq), row(nq), row(nv), row(nv),
                  _full((RET_HEADS, c_, c_)), _full((RET_HEADS, c_, 1)), _full((RET_HEADS, c_, 1)),
                  _full((1, RET_DV))],
        out_specs=row(nv),
        out_shape=jax.ShapeDtypeStruct((b * t, nv), BF16),
        scratch_shapes=[pltpu.VMEM((RET_HEADS, RET_DK, RET_DV), F32)],
        compiler_params=_cparams("parallel", "arbitrary"),
        name="retention",
    )(decay_c, q, k, v, gate, dmat, xi, zeta, rnorm.reshape(1, RET_DV))


def kernel(x, even_attn_norm, even_w_in, even_gla_wa2, even_gla_ba2, even_gla_norm, even_w_out,
           odd_attn_norm, odd_w_in, odd_ret_norm, odd_w_out,
           ffn_norm, ffn_w_gate, ffn_w_up, ffn_w_down, final_norm):
    b, t, d = x.shape
    h = x.reshape(b * t, d)

    gq, gk, gv, la, gr, dk, ik, dqt, iqt, dvt, iwt = _even_proj(
        h, even_attn_norm[0], even_w_in[0], even_gla_wa2[0], even_gla_ba2[0])
    o_gla = _gla(gq, gk, gv, la, gr, even_gla_norm[0], b, t)
    o_dsa = _dsa(iqt, dqt, iwt, ik, dk, dvt, b, t)
    w_out = even_w_out[0].astype(BF16)
    n_gla = GLA_HEADS * GLA_DV
    h = _mix_ffn(h, [o_gla, o_dsa], [w_out[:n_gla], w_out[n_gla:]],
                 ffn_norm[0], ffn_w_gate[0], ffn_w_up[0], ffn_w_down[0])

    q, k, v, gate = _odd_proj(h, odd_attn_norm[0], odd_w_in[0], t)
    o_ret = _retention(q, k, v, gate, odd_ret_norm[0], b, t)
    h = _mix_ffn(h, [o_ret], [odd_w_out[0].astype(BF16)],
                 ffn_norm[1], ffn_w_gate[1], ffn_w_up[1], ffn_w_down[1], final_gain=final_norm)
    return h.reshape(b, t, d)
```

```python
import functools

import jax
import jax.numpy as jnp
from jax import lax
from jax.experimental import pallas as pl
from jax.experimental.pallas import tpu as pltpu

F32 = jnp.float32
BF16 = jnp.bfloat16

D_MODEL = 1024
EPS = 1e-6
GLA_HEADS = 4
GLA_DV = 128
GLA_DK = 64
GLA_RANK = 16
GLA_GATE_NORM = 16.0
GLA_CHUNK = 64
DSA_HEADS = 4
DSA_DH = 128
IDX_HEADS = 8
IDX_DIM = 64
TOPK_MAX = 256
RET_HEADS = 4
RET_DK = 256
RET_DV = 512
ROPE_BASE = 10000.0
D_FF = 2816

LANES = 128
SUBLANES = 8
VMEM_LIMIT = 56 * 1024 * 1024

PROJ_TM = 512
FFN_TM = 512
FFN_CHUNKS = ((0, 1024), (1024, 2048), (2048, 2816))
GLA_TT = 1024
RET_CHUNK = 256
RET_TT = 1024
DSA_TQ = 256
DSA_TK = 256
DSA_SK = 128
DSA_VROWS = DSA_DH + 16
LOG2E = 1.4426950408889634
PACKED_ROWS = 16
BISECT16_ITERS = 8
BISECT_ITERS = 8


def _cparams(*sem):
    return pltpu.CompilerParams(dimension_semantics=sem, vmem_limit_bytes=VMEM_LIMIT)


def _dot(a, b):
    return jnp.dot(a, b, preferred_element_type=F32)


def _dot_nt(a, b):
    return lax.dot_general(a, b, (((1,), (1,)), ((), ())), preferred_element_type=F32)


def _dot_tn(a, b):
    return lax.dot_general(a, b, (((0,), (0,)), ((), ())), preferred_element_type=F32)


def _rms(x, g):
    return x * lax.rsqrt(jnp.mean(x * x, axis=-1, keepdims=True) + EPS) * g


def _silu(x):
    return x * jax.nn.sigmoid(x)


def _full(shape):
    return pl.BlockSpec(shape, lambda *_: (0,) * len(shape))


def _resident(shape):
    return pl.BlockSpec(shape, lambda *_: (0,) * len(shape), pipeline_mode=pl.Buffered(1))


_E_GQ, _E_GK, _E_GV, _E_GA, _E_GR, _E_DK, _E_IK, _E_END = (0, 256, 512, 1024, 1152, 1664, 1792, 1920)
_T_DQ, _T_IQ, _T_DV, _T_IW, _T_END = (0, 512, 1024, 1152, 1168)


def _even_proj_kernel(x_ref, g_ref, w_ref, wt_ref, wa2_ref, ba2_ref,
                      gq_ref, gk_ref, gv_ref, la_ref, gr_ref, dk_ref, ik_ref,
                      dqt_ref, iqt_ref, dvt_ref, iwt_ref):
    xn = _rms(x_ref[...], g_ref[...]).astype(BF16)

    def proj(a, b):
        return _dot(xn, w_ref[:, a:b])

    def proj_t(a, b):
        return _dot_nt(wt_ref[a:b, :], xn)

    ga = proj(_E_GA, _E_GR).astype(BF16)
    gq_ref[...] = proj(_E_GQ, _E_GK)
    gk_ref[...] = proj(_E_GK, _E_GV)
    gv_ref[...] = proj(_E_GV, _E_GA).astype(BF16)
    z = _dot(ga, wa2_ref[...]) + ba2_ref[...]
    la_ref[...] = jax.nn.log_sigmoid(z) * (1.0 / GLA_GATE_NORM)
    gr_ref[...] = proj(_E_GR, _E_DK)
    dk_ref[...] = proj(_E_DK, _E_IK).astype(BF16)
    ik_ref[...] = proj(_E_IK, _E_IK + IDX_DIM).astype(BF16)
    dqt_ref[...] = (proj_t(_T_DQ, _T_IQ) * (DSA_DH ** -0.5 * LOG2E)).astype(BF16)
    iqt_ref[...] = proj_t(_T_IQ, _T_DV).astype(BF16)
    dvt = proj_t(_T_DV, _T_IW).astype(BF16)
    for c in range(dvt_ref.shape[0]):
        dvt_ref[c, 0:DSA_DH, :] = dvt[:, c * DSA_TK:(c + 1) * DSA_TK]
        dvt_ref[c, DSA_DH:DSA_VROWS, :] = jnp.ones((DSA_VROWS - DSA_DH, DSA_TK), BF16)
    iwt_ref[...] = proj_t(_T_IW, _T_END) * ((IDX_HEADS ** -0.5) * (IDX_DIM ** -0.5))


def _even_proj(x2, gain, w_in, wa2, ba2):
    m = x2.shape[0]
    tm = PROJ_TM
    s = [0, 256, 512, 1024, 1040, 1552, 2064, 2192, 2320, 2832, 2896, 2904]
    gq, gk, gv, ga, gr, dq, dk, dv, iq, ik, iw = [w_in[:, s[i]:s[i + 1]] for i in range(11)]
    zpad = lambda n: jnp.zeros((D_MODEL, n), w_in.dtype)
    w = jnp.concatenate([gq, gk, gv, ga, zpad(LANES - GLA_RANK), gr, dk, ik, zpad(LANES - IDX_DIM)],
                        axis=1).astype(BF16)
    wt = jnp.concatenate([dq, iq, dv, iw, zpad(_T_END - _T_IW - IDX_HEADS)], axis=1).T.astype(BF16)
    wa2p = jnp.concatenate(
        [wa2, jnp.zeros((LANES - GLA_RANK, wa2.shape[1]), wa2.dtype)], axis=0).astype(BF16)
    row = lambda n: pl.BlockSpec((tm, n), lambda i: (i, 0))
    col = lambda n: pl.BlockSpec((n, tm), lambda i: (0, i))
    n_iw = _T_END - _T_IW
    return pl.pallas_call(
        _even_proj_kernel,
        grid=(m // tm,),
        in_specs=[row(D_MODEL), _full((1, D_MODEL)), _full((D_MODEL, _E_END)),
                  _full((_T_END, D_MODEL)), _full((LANES, 256)), _full((1, 256))],
        out_specs=[row(256), row(256), row(512), row(256), row(512), row(DSA_DH), row(IDX_DIM),
                   col(512), col(512),
                   pl.BlockSpec((tm // DSA_TK, DSA_VROWS, DSA_TK), lambda i: (i, 0, 0)),
                   col(n_iw)],
        out_shape=[jax.ShapeDtypeStruct((m, 256), F32), jax.ShapeDtypeStruct((m, 256), F32),
                   jax.ShapeDtypeStruct((m, 512), BF16), jax.ShapeDtypeStruct((m, 256), F32),
                   jax.ShapeDtypeStruct((m, 512), F32), jax.ShapeDtypeStruct((m, DSA_DH), BF16),
                   jax.ShapeDtypeStruct((m, IDX_DIM), BF16),
                   jax.ShapeDtypeStruct((512, m), BF16), jax.ShapeDtypeStruct((512, m), BF16),
                   jax.ShapeDtypeStruct((m // DSA_TK, DSA_VROWS, DSA_TK), BF16),
                   jax.ShapeDtypeStruct((n_iw, m), F32)],
        compiler_params=_cparams("parallel"),
        name="even_proj",
    )(x2, gain.reshape(1, D_MODEL), w, wt, wa2p, ba2.reshape(1, 256))


def _gla_kernel(q_ref, k_ref, v_ref, la_ref, gr_ref, gn_ref, o_ref, st_ref, *, n_chunks):
    c_ = GLA_CHUNK

    @pl.when(pl.program_id(1) == 0)
    def _():
        st_ref[...] = jnp.zeros_like(st_ref)

    ri = lax.broadcasted_iota(jnp.int32, (c_, c_), 0)
    ci = lax.broadcasted_iota(jnp.int32, (c_, c_), 1)
    tri = ri >= ci
    ltri = jnp.where(tri, 1.0, 0.0).astype(BF16)
    gn = gn_ref[...]

    chunks = range(n_chunks)
    heads = range(GLA_HEADS)
    rows = [slice(c * c_, (c + 1) * c_) for c in chunks]
    vs = [slice(h * GLA_DV, (h + 1) * GLA_DV) for h in heads]

    def cumsum(g):
        g1 = g.astype(BF16)
        r1 = g - g1.astype(F32)
        g2 = r1.astype(BF16)
        g3 = (r1 - g2.astype(F32)).astype(BF16)
        return _dot(ltri, g1) + _dot(ltri, g2) + _dot(ltri, g3)

    low_half = lax.broadcasted_iota(jnp.int32, (c_, LANES), 1) < GLA_DK
    pair = [slice((h // 2) * LANES, (h // 2 + 1) * LANES) for h in heads]

    def half(z, h):
        zp = z[:, pair[h]]
        return (jnp.where(low_half, zp, 0.0) if h % 2 == 0 else jnp.where(low_half, 0.0, zp)).astype(BF16)

    cum = [cumsum(la_ref[rows[c], :]) for c in chunks]
    last = [cum[c][c_ - 1:c_, :] for c in chunks]
    q_t = [(q_ref[rows[c], :] * jnp.exp(cum[c]) * (GLA_DK ** -0.5)).astype(BF16) for c in chunks]
    k_t = [k_ref[rows[c], :] * jnp.exp(-cum[c]) for c in chunks]
    k_e = [k_ref[rows[c], :] * jnp.exp(last[c] - cum[c]) for c in chunks]
    dec = [jnp.exp(last[c]) for c in chunks]
    a = [[jnp.where(tri, _dot_nt(q_t[c][:, pair[h]], half(k_t[c], h)), 0.0).astype(BF16)
          for h in heads] for c in chunks]
    o_intra = [[_dot(a[c][h], v_ref[rows[c], vs[h]]) for h in heads] for c in chunks]
    kv = [[_dot_tn(v_ref[rows[c], vs[h]], half(k_e[c], h)) for h in heads] for c in chunks]
    for h in heads:
        st = st_ref[h]
        for c in chunks:
            o = o_intra[c][h] + _dot_nt(q_t[c][:, pair[h]], st.astype(BF16))
            st = dec[c][:, pair[h]] * st + kv[c][h]
            y = _rms(o, gn) * _silu(gr_ref[rows[c], vs[h]])
            o_ref[rows[c], vs[h]] = y.astype(o_ref.dtype)
        st_ref[h] = st


def _gla(gq, gk, gv, la, gr, gnorm, b, t):
    tt = min(GLA_TT, t)
    nt = t // tt
    row = lambda n: pl.BlockSpec((tt, n), lambda i, j: (i * nt + j, 0))
    return pl.pallas_call(
        functools.partial(_gla_kernel, n_chunks=tt // GLA_CHUNK),
        grid=(b, nt),
        in_specs=[row(256), row(256), row(512), row(256), row(512), _full((1, GLA_DV))],
        out_specs=row(512),
        out_shape=jax.ShapeDtypeStruct((b * t, 512), BF16),
        scratch_shapes=[pltpu.VMEM((GLA_HEADS, GLA_DV, LANES), F32)],
        compiler_params=_cparams("parallel", "arbitrary"),
        name="gla",
    )(gq, gk, gv, la, gr, gnorm.reshape(1, GLA_DV))


def _dsa_kernel(iqt_ref, dqt_ref, iwt_ref, ik_ref, dk_ref, dvt_ref, o_ref,
                s_ref, sb_ref, acc_ref, bias_ref, sbuf_ref, *, topk, t_len):
    tq, tk, sk = DSA_TQ, DSA_TK, DSA_SK
    n_sub = tk // sk
    qi = pl.program_id(1)
    q0 = qi * tq
    nkt = (q0 + tq + tk - 1) // tk
    neg_inf = -jnp.inf

    qpos = q0 + lax.broadcasted_iota(jnp.int32, (sk, tq), 1)
    krow = lax.broadcasted_iota(jnp.int32, (sk, tq), 0)
    krow_t = lax.broadcasted_iota(jnp.int32, (tk, tq), 0)

    def colsum(c):
        r = c.shape[0] // (4 * SUBLANES)
        part = jnp.sum(c.reshape(r, 4, SUBLANES, tq), axis=0)
        return jnp.sum(part, axis=0)

    def score_tile(kt, carry, diagonal):
        rmax, rmin = carry
        for j in range(n_sub):
            k0 = pl.multiple_of(kt * tk, tk) + j * sk
            ik_t = ik_ref[pl.ds(k0, sk), :]
            sc = jnp.zeros((sk, tq), F32)
            for h in range(IDX_HEADS):
                lg = _dot(ik_t, iqt_ref[h * IDX_DIM:(h + 1) * IDX_DIM, :])
                sc = sc + iwt_ref[h:h + 1, :] * jnp.maximum(lg, 0.0)
            if diagonal:
                allowed = (k0 + krow) <= qpos
                hi_part, lo_part = jnp.where(allowed, sc, neg_inf), jnp.where(allowed, sc, jnp.inf)
            else:
                hi_part, lo_part = sc, sc
            s_ref[kt, j * sk:(j + 1) * sk, :] = hi_part
            sb_ref[kt, j * sk:(j + 1) * sk, :] = hi_part.astype(BF16)
            rmax = jnp.maximum(rmax, jnp.max(hi_part.reshape(sk // SUBLANES, SUBLANES, tq), axis=0))
            rmin = jnp.minimum(rmin, jnp.min(lo_part.reshape(sk // SUBLANES, SUBLANES, tq), axis=0))
        return rmax, rmin

    def score_pair(i, carry):
        carry = score_tile(2 * i, carry, False)
        return score_tile(jnp.minimum(2 * i + 1, nkt - 2), carry, False)

    carry = lax.fori_loop(0, nkt // 2, score_pair,
                          (jnp.full((SUBLANES, tq), neg_inf, F32), jnp.full((SUBLANES, tq), jnp.inf, F32)))
    rmax, rmin = score_tile(nkt - 1, carry, True)
    rmax = jnp.max(rmax, axis=0, keepdims=True)
    rmin = jnp.min(rmin, axis=0, keepdims=True)

    def over_tiles(body, init):
        carry = lax.fori_loop(0, nkt // 2, lambda i, c: body(2 * i + 1, body(2 * i, c)), init)
        return lax.cond(nkt % 2 == 1, lambda: body(nkt - 1, carry), lambda: carry)

    def count(ind_fn):
        def body(kt, acc):
            return acc + colsum(ind_fn(s_ref[kt], kt))
        acc = over_tiles(body, jnp.zeros((SUBLANES, tq), F32))
        return jnp.sum(acc, axis=0, keepdims=True)

    kf = jnp.float32(topk)

    def count16(mid_b):
        def body(kt, acc):
            ind = jnp.where(sb_ref[kt] >= mid_b, jnp.ones((), BF16), jnp.zeros((), BF16))
            parts = [ind[r * PACKED_ROWS:(r + 1) * PACKED_ROWS, :] for r in range(tk // PACKED_ROWS)]
            while len(parts) > 1:
                parts = [parts[i] + parts[i + 1] for i in range(0, len(parts), 2)]
            return acc + parts[0].astype(F32)
        acc = over_tiles(body, jnp.zeros((PACKED_ROWS, tq), F32))
        return jnp.sum(acc, axis=0, keepdims=True)

    def bisect16(_, carry):
        lo, hi = carry
        mid_b = (lo + (hi - lo) * 0.5).astype(BF16)
        mid = mid_b.astype(F32)
        ge = count16(mid_b) >= kf
        below = mid - jnp.maximum(jnp.abs(mid) * 2.0 ** -7, 1e-30)
        return (jnp.where(ge, jnp.maximum(lo, below), lo), jnp.where(ge, hi, jnp.minimum(hi, mid)))

    def bisect(_, carry):
        lo, hi = carry
        mid = lo + (hi - lo) * 0.5
        ge = count(lambda s, kt: jnp.where(s >= mid, 1.0, 0.0)) >= kf
        return jnp.where(ge, mid, lo), jnp.where(ge, hi, mid)

    lo, hi = lax.fori_loop(0, BISECT16_ITERS, bisect16, (rmin, rmax))
    lo, _ = lax.fori_loop(0, BISECT_ITERS, bisect, (lo, hi))

    def colmin(c):
        return jnp.min(c.reshape(tk // SUBLANES, SUBLANES, tq), axis=0)

    def snap(kt, carry):
        cnt, mn = carry
        s = s_ref[kt]
        ge = s >= lo
        return cnt + colsum(jnp.where(ge, 1.0, 0.0)), jnp.minimum(mn, colmin(jnp.where(ge, s, jnp.inf)))

    c_lo, v0 = over_tiles(snap, (jnp.zeros((SUBLANES, tq), F32), jnp.full((SUBLANES, tq), jnp.inf, F32)))
    c_lo = jnp.sum(c_lo, axis=0, keepdims=True)
    v0 = jnp.min(v0, axis=0, keepdims=True)

    def count_above(v):
        return count(lambda s, kt: jnp.where(s > v, 1.0, 0.0))

    def next_above(v):
        def body(kt, mn):
            s = s_ref[kt]
            return jnp.minimum(mn, colmin(jnp.where(s > v, s, jnp.inf)))
        return jnp.min(over_tiles(body, jnp.full((SUBLANES, tq), jnp.inf, F32)), axis=0, keepdims=True)

    def pending(n_gt):
        return jnp.max(jnp.where(n_gt < kf, 0.0, 1.0))

    def step_up(carry):
        v, f_v, n_gt, _ = carry
        fin = n_gt < kf
        v = jnp.where(fin, v, next_above(v))
        f_v = jnp.where(fin, f_v, n_gt)
        n_gt = count_above(v)
        return v, f_v, n_gt, pending(n_gt)

    n_gt0 = count_above(v0)
    thr, f_thr, n_gt, _ = lax.while_loop(
        lambda carry: carry[3] > 0.0, step_up, (v0, c_lo, n_gt0, pending(n_gt0)))
    n_eq = f_thr - n_gt
    need = kf - n_gt

    @pl.when(jnp.max(n_eq - need) > 0.0)
    def _():
        def ibisect(_, carry):
            ilo, ihi = carry
            imid = jnp.right_shift(ilo + ihi, 1)
            c = count(lambda s, kt: jnp.where(
                s == thr, jnp.where((kt * tk + krow_t) <= imid, 1.0, 0.0), 0.0))
            ge = c >= need
            return jnp.where(ge, ilo, imid), jnp.where(ge, imid, ihi)
        ilo0 = jnp.full((1, tq), -1, jnp.int32)
        ihi0 = jnp.full((1, tq), t_len - 1, jnp.int32)
        n_it = max(1, int(t_len - 1).bit_length() + 1)
        jlim = lax.fori_loop(0, n_it, ibisect, (ilo0, ihi0))[1]

        def drop(kt, carry):
            s = s_ref[kt]
            s_ref[kt] = jnp.where(s == thr, jnp.where((kt * tk + krow_t) <= jlim, s, neg_inf), s)
            return carry
        lax.fori_loop(0, nkt, drop, 0)

    acc_ref[...] = jnp.zeros(acc_ref.shape, F32)

    def logits(t, slot):
        tc = jnp.minimum(t, nkt - 1)
        thr_eff = jnp.where(t < nkt, thr, jnp.inf)
        bias_ref[slot] = jnp.where(s_ref[tc] >= thr_eff, 0.0, neg_inf)
        k_t = dk_ref[pl.ds(pl.multiple_of(tc * tk, tk), tk), :]
        for h in range(DSA_HEADS):
            sbuf_ref[slot, h] = _dot(k_t, dqt_ref[h * DSA_DH:(h + 1) * DSA_DH, :]) + bias_ref[slot]

    def softmax_pv(t, slot, ms):
        ms = list(ms)
        vt_t = dvt_ref[jnp.minimum(t, nkt - 1)]
        for h in range(DSA_HEADS):
            m_new = jnp.maximum(ms[h], jnp.max(sbuf_ref[slot, h], axis=0, keepdims=True))
            m_safe = jnp.where(m_new == neg_inf, 0.0, m_new)
            alpha = jnp.exp2(ms[h] - m_safe)
            p = jnp.exp2(sbuf_ref[slot, h] - m_safe).astype(BF16)
            acc_ref[h] = alpha * acc_ref[h] + _dot(vt_t, p)
            ms[h] = m_new
        return tuple(ms)

    def attend(i, ms):
        logits(2 * i + 1, 1)
        ms = softmax_pv(2 * i, 0, ms)
        logits(2 * i + 2, 0)
        return softmax_pv(2 * i + 1, 1, ms)

    logits(0, 0)
    m0 = tuple(jnp.full((1, tq), neg_inf, F32) for _ in range(DSA_HEADS))
    lax.fori_loop(0, (nkt + 1) // 2, attend, m0)
    for h in range(DSA_HEADS):
        num = acc_ref[h, 0:DSA_DH, :]
        den = acc_ref[h, DSA_DH:DSA_DH + 1, :]
        o_ref[:, h * DSA_DH:(h + 1) * DSA_DH] = (num / den).T.astype(o_ref.dtype)


def _dsa(iqt, dqt, iwt, ik, dk, dvt, b, t):
    tq, tk, sk = DSA_TQ, DSA_TK, DSA_SK
    assert tq == tk and t % tq == 0
    nq = t // tq
    topk = min(TOPK_MAX, t // 4)
    qcol = lambda n: pl.BlockSpec((n, tq), lambda i, j: (0, i * nq + j))
    krow = lambda n: pl.BlockSpec((t, n), lambda i, j: (i, 0))
    return pl.pallas_call(
        functools.partial(_dsa_kernel, topk=topk, t_len=t),
        grid=(b, nq),
        in_specs=[qcol(512), qcol(512), qcol(iwt.shape[0]), krow(IDX_DIM), krow(DSA_DH),
                  pl.BlockSpec((t // tk, DSA_VROWS, tk), lambda i, j: (i, 0, 0))],
        out_specs=pl.BlockSpec((tq, 512), lambda i, j: (i * nq + j, 0)),
        out_shape=jax.ShapeDtypeStruct((b * t, 512), BF16),
        scratch_shapes=[
            pltpu.VMEM((t // tk, tk, tq), F32),
            pltpu.VMEM((t // tk, tk, tq), BF16),
            pltpu.VMEM((DSA_HEADS, DSA_VROWS, tq), F32),
            pltpu.VMEM((2, tk, tq), F32),
            pltpu.VMEM((2, DSA_HEADS, tk, tq), F32),
        ],
        compiler_params=_cparams("parallel", "arbitrary"),
        name="dsa",
    )(iqt, dqt, iwt, ik, dk, dvt)


def _mix_ffn_kernel(*refs, n_in, final_norm):
    res_ref = refs[0]
    a_refs = refs[1:1 + n_in]
    w_refs = refs[1 + n_in:1 + 2 * n_in]
    g_ref, wg_ref, wu_ref, wd_ref, fg_ref, o_ref, x_ref = refs[1 + 2 * n_in:]
    x = res_ref[...]
    for a, w in zip(a_refs, w_refs):
        x = x + _dot(a[...], w[...])
    x_ref[...] = x
    xn = _rms(x, g_ref[...]).astype(BF16)
    acc = None
    for a, b in FFN_CHUNKS:
        t = _silu(_dot(xn, wg_ref[:, a:b])) * _dot(xn, wu_ref[:, a:b])
        d = _dot(t.astype(BF16), wd_ref[a:b, :])
        acc = d if acc is None else acc + d
    y = x_ref[...] + acc
    o_ref[...] = _rms(y, fg_ref[...]) if final_norm else y


def _mix_ffn(res, acts, w_outs, gain, wg, wu, wd, final_gain=None):
    m = res.shape[0]
    tm = FFN_TM
    fg = jnp.ones((1, D_MODEL), F32) if final_gain is None else final_gain.reshape(1, D_MODEL)
    row = lambda n: pl.BlockSpec((tm, n), lambda i: (i, 0))
    return pl.pallas_call(
        functools.partial(_mix_ffn_kernel, n_in=len(acts), final_norm=final_gain is not None),
        grid=(m // tm,),
        in_specs=[row(D_MODEL)] + [row(a.shape[1]) for a in acts] + [_resident(w.shape) for w in w_outs]
                 + [_full((1, D_MODEL)), _resident((D_MODEL, D_FF)), _resident((D_MODEL, D_FF)),
                    _resident((D_FF, D_MODEL)), _full((1, D_MODEL))],
        out_specs=row(D_MODEL),
        out_shape=jax.ShapeDtypeStruct((m, D_MODEL), F32),
        scratch_shapes=[pltpu.VMEM((tm, D_MODEL), F32)],
        compiler_params=_cparams("parallel"),
        name="mix_ffn",
    )(res, *acts, *w_outs, gain.reshape(1, D_MODEL), wg.astype(BF16), wu.astype(BF16), wd.astype(BF16), fg)


def _odd_proj_kernel(x_ref, g_ref, w_ref, cos_ref, sin_ref, q_ref, k_ref, v_ref, gate_ref):
    xn = _rms(x_ref[...], g_ref[...]).astype(BF16)
    cos = cos_ref[...]
    sin = sin_ref[...]
    half = RET_DK // 2

    def rot(z, out_ref, scale):
        for h in range(RET_HEADS):
            x1 = z[:, h * RET_DK:h * RET_DK + half]
            x2 = z[:, h * RET_DK + half:(h + 1) * RET_DK]
            out_ref[:, h * RET_DK:h * RET_DK + half] = ((x1 * cos - x2 * sin) * scale).astype(BF16)
            out_ref[:, h * RET_DK + half:(h + 1) * RET_DK] = ((x2 * cos + x1 * sin) * scale).astype(BF16)

    nq = RET_HEADS * RET_DK
    nv = RET_HEADS * RET_DV
    rot(_dot(xn, w_ref[:, 0:nq]), q_ref, 1.0)
    rot(_dot(xn, w_ref[:, nq:2 * nq]), k_ref, RET_DK ** -0.5)
    v_ref[...] = _dot(xn, w_ref[:, 2 * nq:2 * nq + nv]).astype(BF16)
    gate_ref[...] = _dot(xn, w_ref[:, 2 * nq + nv:2 * nq + 2 * nv])


def _odd_proj(h, gain, w_in, t):
    m = h.shape[0]
    tm = PROJ_TM
    nt = t // tm
    half = RET_DK // 2
    inv = ROPE_BASE ** (-jnp.arange(half, dtype=F32) / half)
    ang = jnp.arange(t, dtype=jnp.int32).astype(F32)[:, None] * inv[None, :]
    row = lambda n: pl.BlockSpec((tm, n), lambda i: (i, 0))
    pos = pl.BlockSpec((tm, half), lambda i: (i % nt, 0))
    nq = RET_HEADS * RET_DK
    nv = RET_HEADS * RET_DV
    return pl.pallas_call(
        _odd_proj_kernel,
        grid=(m // tm,),
        in_specs=[row(D_MODEL), _full((1, D_MODEL)), _full(w_in.shape), pos, pos],
        out_specs=[row(nq), row(nq), row(nv), row(nv)],
        out_shape=[jax.ShapeDtypeStruct((m, nq), BF16), jax.ShapeDtypeStruct((m, nq), BF16),
                   jax.ShapeDtypeStruct((m, nv), BF16), jax.ShapeDtypeStruct((m, nv), F32)],
        compiler_params=_cparams("parallel"),
        name="odd_proj",
    )(h, gain.reshape(1, D_MODEL), w_in.astype(BF16), jnp.cos(ang), jnp.sin(ang))


def _ret_kernel(dec_ref, q_ref, k_ref, v_ref, gate_ref, dm_ref, xi_ref, ze_ref, rn_ref,
                o_ref, st_ref, *, n_chunks):
    c_ = RET_CHUNK

    @pl.when(pl.program_id(1) == 0)
    def _():
        st_ref[...] = jnp.zeros_like(st_ref)

    rn = rn_ref[...]
    for c in range(n_chunks):
        rows = slice(c * c_, (c + 1) * c_)
        for h in range(RET_HEADS):
            ks = slice(h * RET_DK, (h + 1) * RET_DK)
            vs = slice(h * RET_DV, (h + 1) * RET_DV)
            q = q_ref[rows, ks]
            k = k_ref[rows, ks]
            v = v_ref[rows, vs]
            st = st_ref[h]
            inner = (_dot_nt(q, k) * dm_ref[h]).astype(BF16)
            o = _dot(inner, v) + xi_ref[h] * _dot(q, st.astype(BF16))
            kz = (k.astype(F32) * ze_ref[h]).astype(BF16)
            st_ref[h] = dec_ref[h] * st + _dot_tn(kz, v)
            y = _rms(o, rn) * _silu(gate_ref[rows, vs])
            o_ref[rows, vs] = y.astype(o_ref.dtype)


def _retention(q, k, v, gate, rnorm, b, t):
    c_ = min(RET_CHUNK, t)
    tt = min(RET_TT, t)
    nt = t // tt
    log_g = jnp.log1p(-jnp.exp2(-5.0 - jnp.arange(RET_HEADS, dtype=F32)))
    idx = jnp.arange(c_, dtype=F32)
    rel = idx[:, None] - idx[None, :]
    dmat = jnp.where(rel >= 0, jnp.exp(log_g[:, None, None] * jnp.maximum(rel, 0.0)), 0.0)
    xi = jnp.exp(log_g[:, None] * (idx[None, :] + 1.0))[:, :, None]
    zeta = jnp.exp(log_g[:, None] * (c_ - 1.0 - idx[None, :]))[:, :, None]
    decay_c = jnp.exp(log_g * c_)
    nq = RET_HEADS * RET_DK
    nv = RET_HEADS * RET_DV
    row = lambda n: pl.BlockSpec((tt, n), lambda i, j: (i * nt + j, 0))
    return pl.pallas_call(
        functools.partial(_ret_kernel, n_chunks=tt // c_),
        grid=(b, nt),
        in_specs=[pl.BlockSpec(memory_space=pltpu.SMEM),
                  row(nq), row(nq), row(nv), row(nv),
                  _full((RET_HEADS, c_, c_)), _full((RET_HEADS, c_, 1)), _full((RET_HEADS, c_, 1)),
                  _full((1, RET_DV))],
        out_specs=row(nv),
        out_shape=jax.ShapeDtypeStruct((b * t, nv), BF16),
        scratch_shapes=[pltpu.VMEM((RET_HEADS, RET_DK, RET_DV), F32)],
        compiler_params=_cparams("parallel", "arbitrary"),
        name="retention",
    )(decay_c, q, k, v, gate, dmat, xi, zeta, rnorm.reshape(1, RET_DV))


def kernel(x, even_attn_norm, even_w_in, even_gla_wa2, even_gla_ba2, even_gla_norm, even_w_out,
           odd_attn_norm, odd_w_in, odd_ret_norm, odd_w_out,
           ffn_norm, ffn_w_gate, ffn_w_up, ffn_w_down, final_norm):
    b, t, d = x.shape
    h = x.reshape(b * t, d)

    gq, gk, gv, la, gr, dk, ik, dqt, iqt, dvt, iwt = _even_proj(
        h, even_attn_norm[0], even_w_in[0], even_gla_wa2[0], even_gla_ba2[0])
    o_gla = _gla(gq, gk, gv, la, gr, even_gla_norm[0], b, t)
    o_dsa = _dsa(iqt, dqt, iwt, ik, dk, dvt, b, t)
    w_out = even_w_out[0].astype(BF16)
    n_gla = GLA_HEADS * GLA_DV
    h = _mix_ffn(h, [o_gla, o_dsa], [w_out[:n_gla], w_out[n_gla:]],
                 ffn_norm[0], ffn_w_gate[0], ffn_w_up[0], ffn_w_down[0])

    q, k, v, gate = _odd_proj(h, odd_attn_norm[0], odd_w_in[0], t)
    o_ret = _retention(q, k, v, gate, odd_ret_norm[0], b, t)
    h = _mix_ffn(h, [o_ret], [odd_w_out[0].astype(BF16)],
                 ffn_norm[1], ffn_w_gate[1], ffn_w_up[1], ffn_w_down[1], final_gain=final_norm)
    return h.reshape(b, t, d)
```

```python
import functools

import jax
import jax.numpy as jnp
from jax import lax
from jax.experimental import pallas as pl
from jax.experimental.pallas import tpu as pltpu

F32 = jnp.float32
BF16 = jnp.bfloat16

D_MODEL = 1024
EPS = 1e-6
GLA_HEADS = 4
GLA_DV = 128
GLA_DK = 64
GLA_RANK = 16
GLA_GATE_NORM = 16.0
GLA_CHUNK = 64
DSA_HEADS = 4
DSA_DH = 128
IDX_HEADS = 8
IDX_DIM = 64
TOPK_MAX = 256
RET_HEADS = 4
RET_DK = 256
RET_DV = 512
ROPE_BASE = 10000.0
D_FF = 2816

LANES = 128
SUBLANES = 8
VMEM_LIMIT = 56 * 1024 * 1024

PROJ_TM = 512
FFN_TM = 512
FFN_CHUNKS = ((0, 1024), (1024, 2048), (2048, 2816))
GLA_TT = 1024
RET_CHUNK = 256
RET_TT = 1024
DSA_TQ = 256
DSA_TK = 256
DSA_SK = 128
DSA_VROWS = DSA_DH + 16
LOG2E = 1.4426950408889634
PACKED_ROWS = 16
BISECT16_ITERS = 8
BISECT_ITERS = 8


def _cparams(*sem):
    return pltpu.CompilerParams(dimension_semantics=sem, vmem_limit_bytes=VMEM_LIMIT)


def _dot(a, b):
    return jnp.dot(a, b, preferred_element_type=F32)


def _dot_nt(a, b):
    return lax.dot_general(a, b, (((1,), (1,)), ((), ())), preferred_element_type=F32)


def _dot_tn(a, b):
    return lax.dot_general(a, b, (((0,), (0,)), ((), ())), preferred_element_type=F32)


def _rms(x, g):
    return x * lax.rsqrt(jnp.mean(x * x, axis=-1, keepdims=True) + EPS) * g


def _silu(x):
    return x * jax.nn.sigmoid(x)


def _full(shape):
    return pl.BlockSpec(shape, lambda *_: (0,) * len(shape))


def _resident(shape):
    return pl.BlockSpec(shape, lambda *_: (0,) * len(shape), pipeline_mode=pl.Buffered(1))


_E_GQ, _E_GK, _E_GV, _E_GA, _E_GR, _E_DK, _E_IK, _E_END = (0, 256, 512, 1024, 1152, 1664, 1792, 1920)
_T_DQ, _T_IQ, _T_DV, _T_IW, _T_END = (0, 512, 1024, 1152, 1168)


def _even_proj_kernel(x_ref, g_ref, w_ref, wt_ref, wa2_ref, ba2_ref,
                      gq_ref, gk_ref, gv_ref, la_ref, gr_ref, dk_ref, ik_ref,
                      dqt_ref, iqt_ref, dvt_ref, iwt_ref):
    xn = _rms(x_ref[...], g_ref[...]).astype(BF16)

    def proj(a, b):
        return _dot(xn, w_ref[:, a:b])

    def proj_t(a, b):
        return _dot_nt(wt_ref[a:b, :], xn)

    ga = proj(_E_GA, _E_GR).astype(BF16)
    gq_ref[...] = proj(_E_GQ, _E_GK)
    gk_ref[...] = proj(_E_GK, _E_GV)
    gv_ref[...] = proj(_E_GV, _E_GA).astype(BF16)
    z = _dot(ga, wa2_ref[...]) + ba2_ref[...]
    la_ref[...] = jax.nn.log_sigmoid(z) * (1.0 / GLA_GATE_NORM)
    gr_ref[...] = proj(_E_GR, _E_DK)
    dk_ref[...] = proj(_E_DK, _E_IK).astype(BF16)
    ik_ref[...] = proj(_E_IK, _E_IK + IDX_DIM).astype(BF16)
    dqt_ref[...] = (proj_t(_T_DQ, _T_IQ) * (DSA_DH ** -0.5 * LOG2E)).astype(BF16)
    iqt_ref[...] = proj_t(_T_IQ, _T_DV).astype(BF16)
    dvt = proj_t(_T_DV, _T_IW).astype(BF16)
    for c in range(dvt_ref.shape[0]):
        dvt_ref[c, 0:DSA_DH, :] = dvt[:, c * DSA_TK:(c + 1) * DSA_TK]
        dvt_ref[c, DSA_DH:DSA_VROWS, :] = jnp.ones((DSA_VROWS - DSA_DH, DSA_TK), BF16)
    iwt_ref[...] = proj_t(_T_IW, _T_END) * ((IDX_HEADS ** -0.5) * (IDX_DIM ** -0.5))


def _even_proj(x2, gain, w_in, wa2, ba2):
    m = x2.shape[0]
    tm = PROJ_TM
    s = [0, 256, 512, 1024, 1040, 1552, 2064, 2192, 2320, 2832, 2896, 2904]
    gq, gk, gv, ga, gr, dq, dk, dv, iq, ik, iw = [w_in[:, s[i]:s[i + 1]] for i in range(11)]
    zpad = lambda n: jnp.zeros((D_MODEL, n), w_in.dtype)
    w = jnp.concatenate([gq, gk, gv, ga, zpad(LANES - GLA_RANK), gr, dk, ik, zpad(LANES - IDX_DIM)],
                        axis=1).astype(BF16)
    wt = jnp.concatenate([dq, iq, dv, iw, zpad(_T_END - _T_IW - IDX_HEADS)], axis=1).T.astype(BF16)
    wa2p = jnp.concatenate(
        [wa2, jnp.zeros((LANES - GLA_RANK, wa2.shape[1]), wa2.dtype)], axis=0).astype(BF16)
    row = lambda n: pl.BlockSpec((tm, n), lambda i: (i, 0))
    col = lambda n: pl.BlockSpec((n, tm), lambda i: (0, i))
    n_iw = _T_END - _T_IW
    return pl.pallas_call(
        _even_proj_kernel,
        grid=(m // tm,),
        in_specs=[row(D_MODEL), _full((1, D_MODEL)), _full((D_MODEL, _E_END)),
                  _full((_T_END, D_MODEL)), _full((LANES, 256)), _full((1, 256))],
        out_specs=[row(256), row(256), row(512), row(256), row(512), row(DSA_DH), row(IDX_DIM),
                   col(512), col(512),
                   pl.BlockSpec((tm // DSA_TK, DSA_VROWS, DSA_TK), lambda i: (i, 0, 0)),
                   col(n_iw)],
        out_shape=[jax.ShapeDtypeStruct((m, 256), F32), jax.ShapeDtypeStruct((m, 256), F32),
                   jax.ShapeDtypeStruct((m, 512), BF16), jax.ShapeDtypeStruct((m, 256), F32),
                   jax.ShapeDtypeStruct((m, 512), F32), jax.ShapeDtypeStruct((m, DSA_DH), BF16),
                   jax.ShapeDtypeStruct((m, IDX_DIM), BF16),
                   jax.ShapeDtypeStruct((512, m), BF16), jax.ShapeDtypeStruct((512, m), BF16),
                   jax.ShapeDtypeStruct((m // DSA_TK, DSA_VROWS, DSA_TK), BF16),
                   jax.ShapeDtypeStruct((n_iw, m), F32)],
        compiler_params=_cparams("parallel"),
        name="even_proj",
    )(x2, gain.reshape(1, D_MODEL), w, wt, wa2p, ba2.reshape(1, 256))


def _gla_kernel(q_ref, k_ref, v_ref, la_ref, gr_ref, gn_ref, o_ref, st_ref, *, n_chunks):
    c_ = GLA_CHUNK

    @pl.when(pl.program_id(1) == 0)
    def _():
        st_ref[...] = jnp.zeros_like(st_ref)

    ri = lax.broadcasted_iota(jnp.int32, (c_, c_), 0)
    ci = lax.broadcasted_iota(jnp.int32, (c_, c_), 1)
    tri = ri >= ci
    ltri = jnp.where(tri, 1.0, 0.0).astype(BF16)
    gn = gn_ref[...]

    chunks = range(n_chunks)
    heads = range(GLA_HEADS)
    rows = [slice(c * c_, (c + 1) * c_) for c in chunks]
    vs = [slice(h * GLA_DV, (h + 1) * GLA_DV) for h in heads]

    def cumsum(g):
        g1 = g.astype(BF16)
        r1 = g - g1.astype(F32)
        g2 = r1.astype(BF16)
        g3 = (r1 - g2.astype(F32)).astype(BF16)
        return _dot(ltri, g1) + _dot(ltri, g2) + _dot(ltri, g3)

    low_half = lax.broadcasted_iota(jnp.int32, (c_, LANES), 1) < GLA_DK
    pair = [slice((h // 2) * LANES, (h // 2 + 1) * LANES) for h in heads]

    def half(z, h):
        zp = z[:, pair[h]]
        return (jnp.where(low_half, zp, 0.0) if h % 2 == 0 else jnp.where(low_half, 0.0, zp)).astype(BF16)

    cum = [cumsum(la_ref[rows[c], :]) for c in chunks]
    last = [cum[c][c_ - 1:c_, :] for c in chunks]
    q_t = [(q_ref[rows[c], :] * jnp.exp(cum[c]) * (GLA_DK ** -0.5)).astype(BF16) for c in chunks]
    k_t = [k_ref[rows[c], :] * jnp.exp(-cum[c]) for c in chunks]
    k_e = [k_ref[rows[c], :] * jnp.exp(last[c] - cum[c]) for c in chunks]
    dec = [jnp.exp(last[c]) for c in chunks]
    a = [[jnp.where(tri, _dot_nt(q_t[c][:, pair[h]], half(k_t[c], h)), 0.0).astype(BF16)
          for h in heads] for c in chunks]
    o_intra = [[_dot(a[c][h], v_ref[rows[c], vs[h]]) for h in heads] for c in chunks]
    kv = [[_dot_tn(v_ref[rows[c], vs[h]], half(k_e[c], h)) for h in heads] for c in chunks]
    for h in heads:
        st = st_ref[h]
        for c in chunks:
            o = o_intra[c][h] + _dot_nt(q_t[c][:, pair[h]], st.astype(BF16))
            st = dec[c][:, pair[h]] * st + kv[c][h]
            y = _rms(o, gn) * _silu(gr_ref[rows[c], vs[h]])
            o_ref[rows[c], vs[h]] = y.astype(o_ref.dtype)
        st_ref[h] = st


def _gla(gq, gk, gv, la, gr, gnorm, b, t):
    tt = min(GLA_TT, t)
    nt = t // tt
    row = lambda n: pl.BlockSpec((tt, n), lambda i, j: (i * nt + j, 0))
    return pl.pallas_call(
        functools.partial(_gla_kernel, n_chunks=tt // GLA_CHUNK),
        grid=(b, nt),
        in_specs=[row(256), row(256), row(512), row(256), row(512), _full((1, GLA_DV))],
        out_specs=row(512),
        out_shape=jax.ShapeDtypeStruct((b * t, 512), BF16),
        scratch_shapes=[pltpu.VMEM((GLA_HEADS, GLA_DV, LANES), F32)],
        compiler_params=_cparams("parallel", "arbitrary"),
        name="gla",
    )(gq, gk, gv, la, gr, gnorm.reshape(1, GLA_DV))


def _dsa_kernel(iqt_ref, dqt_ref, iwt_ref, ik_ref, dk_ref, dvt_ref, o_ref,
                s_ref, sb_ref, acc_ref, bias_ref, sbuf_ref, *, topk, t_len):
    tq, tk, sk = DSA_TQ, DSA_TK, DSA_SK
    n_sub = tk // sk
    qi = pl.program_id(1)
    q0 = qi * tq
    nkt = (q0 + tq + tk - 1) // tk
    neg_inf = -jnp.inf

    qpos = q0 + lax.broadcasted_iota(jnp.int32, (sk, tq), 1)
    krow = lax.broadcasted_iota(jnp.int32, (sk, tq), 0)
    krow_t = lax.broadcasted_iota(jnp.int32, (tk, tq), 0)

    def colsum(c):
        r = c.shape[0] // (4 * SUBLANES)
        part = jnp.sum(c.reshape(r, 4, SUBLANES, tq), axis=0)
        return jnp.sum(part, axis=0)

    def score_tile(kt, carry, diagonal):
        rmax, rmin = carry
        for j in range(n_sub):
            k0 = pl.multiple_of(kt * tk, tk) + j * sk
            ik_t = ik_ref[pl.ds(k0, sk), :]
            sc = jnp.zeros((sk, tq), F32)
            for h in range(IDX_HEADS):
                lg = _dot(ik_t, iqt_ref[h * IDX_DIM:(h + 1) * IDX_DIM, :])
                sc = sc + iwt_ref[h:h + 1, :] * jnp.maximum(lg, 0.0)
            if diagonal:
                allowed = (k0 + krow) <= qpos
                hi_part, lo_part = jnp.where(allowed, sc, neg_inf), jnp.where(allowed, sc, jnp.inf)
            else:
                hi_part, lo_part = sc, sc
            s_ref[kt, j * sk:(j + 1) * sk, :] = hi_part
            sb_ref[kt, j * sk:(j + 1) * sk, :] = hi_part.astype(BF16)
            rmax = jnp.maximum(rmax, jnp.max(hi_part.reshape(sk // SUBLANES, SUBLANES, tq), axis=0))
            rmin = jnp.minimum(rmin, jnp.min(lo_part.reshape(sk // SUBLANES, SUBLANES, tq), axis=0))
        return rmax, rmin

    def score_pair(i, carry):
        carry = score_tile(2 * i, carry, False)
        return score_tile(jnp.minimum(2 * i + 1, nkt - 2), carry, False)

    carry = lax.fori_loop(0, nkt // 2, score_pair,
                          (jnp.full((SUBLANES, tq), neg_inf, F32), jnp.full((SUBLANES, tq), jnp.inf, F32)))
    rmax, rmin = score_tile(nkt - 1, carry, True)
    rmax = jnp.max(rmax, axis=0, keepdims=True)
    rmin = jnp.min(rmin, axis=0, keepdims=True)

    def over_tiles(body, init):
        carry = lax.fori_loop(0, nkt // 2, lambda i, c: body(2 * i + 1, body(2 * i, c)), init)
        return lax.cond(nkt % 2 == 1, lambda: body(nkt - 1, carry), lambda: carry)

    def count(ind_fn):
        def body(kt, acc):
            return acc + colsum(ind_fn(s_ref[kt], kt))
        acc = over_tiles(body, jnp.zeros((SUBLANES, tq), F32))
        return jnp.sum(acc, axis=0, keepdims=True)

    kf = jnp.float32(topk)

    def count16(mid_b):
        def body(kt, acc):
            ind = jnp.where(sb_ref[kt] >= mid_b, jnp.ones((), BF16), jnp.zeros((), BF16))
            parts = [ind[r * PACKED_ROWS:(r + 1) * PACKED_ROWS, :] for r in range(tk // PACKED_ROWS)]
            while len(parts) > 1:
                parts = [parts[i] + parts[i + 1] for i in range(0, len(parts), 2)]
            return acc + parts[0].astype(F32)
        acc = over_tiles(body, jnp.zeros((PACKED_ROWS, tq), F32))
        return jnp.sum(acc, axis=0, keepdims=True)

    def bisect16(_, carry):
        lo, hi = carry
        mid_b = (lo + (hi - lo) * 0.5).astype(BF16)
        mid = mid_b.astype(F32)
        ge = count16(mid_b) >= kf
        below = mid - jnp.maximum(jnp.abs(mid) * 2.0 ** -7, 1e-30)
        return (jnp.where(ge, jnp.maximum(lo, below), lo), jnp.where(ge, hi, jnp.minimum(hi, mid)))

    def bisect(_, carry):
        lo, hi = carry
        mid = lo + (hi - lo) * 0.5
        ge = count(lambda s, kt: jnp.where(s >= mid, 1.0, 0.0)) >= kf
        return jnp.where(ge, mid, lo), jnp.where(ge, hi, mid)

    lo, hi = lax.fori_loop(0, BISECT16_ITERS, bisect16, (rmin, rmax))
    lo, _ = lax.fori_loop(0, BISECT_ITERS, bisect, (lo, hi))

    def colmin(c):
        return jnp.min(c.reshape(tk // SUBLANES, SUBLANES, tq), axis=0)

    def snap(kt, carry):
        cnt, mn = carry
        s = s_ref[kt]
        ge = s >= lo
        return cnt + colsum(jnp.where(ge, 1.0, 0.0)), jnp.minimum(mn, colmin(jnp.where(ge, s, jnp.inf)))

    c_lo, v0 = over_tiles(snap, (jnp.zeros((SUBLANES, tq), F32), jnp.full((SUBLANES, tq), jnp.inf, F32)))
    c_lo = jnp.sum(c_lo, axis=0, keepdims=True)
    v0 = jnp.min(v0, axis=0, keepdims=True)

    def count_above(v):
        return count(lambda s, kt: jnp.where(s > v, 1.0, 0.0))

    def next_above(v):
        def body(kt, mn):
            s = s_ref[kt]
            return jnp.minimum(mn, colmin(jnp.where(s > v, s, jnp.inf)))
        return jnp.min(over_tiles(body, jnp.full((SUBLANES, tq), jnp.inf, F32)), axis=0, keepdims=True)

    def pending(n_gt):
        return jnp.max(jnp.where(n_gt < kf, 0.0, 1.0))

    def step_up(carry):
        v, f_v, n_gt, _ = carry
        fin = n_gt < kf
        v = jnp.where(fin, v, next_above(v))
        f_v = jnp.where(fin, f_v, n_gt)
        n_gt = count_above(v)
        return v, f_v, n_gt, pending(n_gt)

    n_gt0 = count_above(v0)
    thr, f_thr, n_gt, _ = lax.while_loop(
        lambda carry: carry[3] > 0.0, step_up, (v0, c_lo, n_gt0, pending(n_gt0)))
    n_eq = f_thr - n_gt
    need = kf - n_gt

    @pl.when(jnp.max(n_eq - need) > 0.0)
    def _():
        def ibisect(_, carry):
            ilo, ihi = carry
            imid = jnp.right_shift(ilo + ihi, 1)
            c = count(lambda s, kt: jnp.where(
                s == thr, jnp.where((kt * tk + krow_t) <= imid, 1.0, 0.0), 0.0))
            ge = c >= need
            return jnp.where(ge, ilo, imid), jnp.where(ge, imid, ihi)
        ilo0 = jnp.full((1, tq), -1, jnp.int32)
        ihi0 = jnp.full((1, tq), t_len - 1, jnp.int32)
        n_it = max(1, int(t_len - 1).bit_length() + 1)
        jlim = lax.fori_loop(0, n_it, ibisect, (ilo0, ihi0))[1]

        def drop(kt, carry):
            s = s_ref[kt]
            s_ref[kt] = jnp.where(s == thr, jnp.where((kt * tk + krow_t) <= jlim, s, neg_inf), s)
            return carry
        lax.fori_loop(0, nkt, drop, 0)

    acc_ref[...] = jnp.zeros(acc_ref.shape, F32)

    def logits(t, slot):
        tc = jnp.minimum(t, nkt - 1)
        thr_eff = jnp.where(t < nkt, thr, jnp.inf)
        bias_ref[slot] = jnp.where(s_ref[tc] >= thr_eff, 0.0, neg_inf)
        k_t = dk_ref[pl.ds(pl.multiple_of(tc * tk, tk), tk), :]
        for h in range(DSA_HEADS):
            sbuf_ref[slot, h] = _dot(k_t, dqt_ref[h * DSA_DH:(h + 1) * DSA_DH, :]) + bias_ref[slot]

    def softmax_pv(t, slot, ms):
        ms = list(ms)
        vt_t = dvt_ref[jnp.minimum(t, nkt - 1)]
        for h in range(DSA_HEADS):
            m_new = jnp.maximum(ms[h], jnp.max(sbuf_ref[slot, h], axis=0, keepdims=True))
            m_safe = jnp.where(m_new == neg_inf, 0.0, m_new)
            alpha = jnp.exp2(ms[h] - m_safe)
            p = jnp.exp2(sbuf_ref[slot, h] - m_safe).astype(BF16)
            acc_ref[h] = alpha * acc_ref[h] + _dot(vt_t, p)
            ms[h] = m_new
        return tuple(ms)

    def attend(i, ms):
        logits(2 * i + 1, 1)
        ms = softmax_pv(2 * i, 0, ms)
        logits(2 * i + 2, 0)
        return softmax_pv(2 * i + 1, 1, ms)

    logits(0, 0)
    m0 = tuple(jnp.full((1, tq), neg_inf, F32) for _ in range(DSA_HEADS))
    lax.fori_loop(0, (nkt + 1) // 2, attend, m0)
    for h in range(DSA_HEADS):
        num = acc_ref[h, 0:DSA_DH, :]
        den = acc_ref[h, DSA_DH:DSA_DH + 1, :]
        o_ref[:, h * DSA_DH:(h + 1) * DSA_DH] = (num / den).T.astype(o_ref.dtype)


def _dsa(iqt, dqt, iwt, ik, dk, dvt, b, t):
    tq, tk, sk = DSA_TQ, DSA_TK, DSA_SK
    assert tq == tk and t % tq == 0
    nq = t // tq
    topk = min(TOPK_MAX, t // 4)
    qcol = lambda n: pl.BlockSpec((n, tq), lambda i, j: (0, i * nq + j))
    krow = lambda n: pl.BlockSpec((t, n), lambda i, j: (i, 0))
    return pl.pallas_call(
        functools.partial(_dsa_kernel, topk=topk, t_len=t),
        grid=(b, nq),
        in_specs=[qcol(512), qcol(512), qcol(iwt.shape[0]), krow(IDX_DIM), krow(DSA_DH),
                  pl.BlockSpec((t // tk, DSA_VROWS, tk), lambda i, j: (i, 0, 0))],
        out_specs=pl.BlockSpec((tq, 512), lambda i, j: (i * nq + j, 0)),
        out_shape=jax.ShapeDtypeStruct((b * t, 512), BF16),
        scratch_shapes=[
            pltpu.VMEM((t // tk, tk, tq), F32),
            pltpu.VMEM((t // tk, tk, tq), BF16),
            pltpu.VMEM((DSA_HEADS, DSA_VROWS, tq), F32),
            pltpu.VMEM((2, tk, tq), F32),
            pltpu.VMEM((2, DSA_HEADS, tk, tq), F32),
        ],
        compiler_params=_cparams("parallel", "arbitrary"),
        name="dsa",
    )(iqt, dqt, iwt, ik, dk, dvt)


def _mix_ffn_kernel(*refs, n_in, final_norm):
    res_ref = refs[0]
    a_refs = refs[1:1 + n_in]
    w_refs = refs[1 + n_in:1 + 2 * n_in]
    g_ref, wg_ref, wu_ref, wd_ref, fg_ref, o_ref, x_ref = refs[1 + 2 * n_in:]
    x = res_ref[...]
    for a, w in zip(a_refs, w_refs):
        x = x + _dot(a[...], w[...])
    x_ref[...] = x
    xn = _rms(x, g_ref[...]).astype(BF16)
    acc = None
    for a, b in FFN_CHUNKS:
        t = _silu(_dot(xn, wg_ref[:, a:b])) * _dot(xn, wu_ref[:, a:b])
        d = _dot(t.astype(BF16), wd_ref[a:b, :])
        acc = d if acc is None else acc + d
    y = x_ref[...] + acc
    o_ref[...] = _rms(y, fg_ref[...]) if final_norm else y


def _mix_ffn(res, acts, w_outs, gain, wg, wu, wd, final_gain=None):
    m = res.shape[0]
    tm = FFN_TM
    fg = jnp.ones((1, D_MODEL), F32) if final_gain is None else final_gain.reshape(1, D_MODEL)
    row = lambda n: pl.BlockSpec((tm, n), lambda i: (i, 0))
    return pl.pallas_call(
        functools.partial(_mix_ffn_kernel, n_in=len(acts), final_norm=final_gain is not None),
        grid=(m // tm,),
        in_specs=[row(D_MODEL)] + [row(a.shape[1]) for a in acts] + [_resident(w.shape) for w in w_outs]
                 + [_full((1, D_MODEL)), _resident((D_MODEL, D_FF)), _resident((D_MODEL, D_FF)),
                    _resident((D_FF, D_MODEL)), _full((1, D_MODEL))],
        out_specs=row(D_MODEL),
        out_shape=jax.ShapeDtypeStruct((m, D_MODEL), F32),
        scratch_shapes=[pltpu.VMEM((tm, D_MODEL), F32)],
        compiler_params=_cparams("parallel"),
        name="mix_ffn",
    )(res, *acts, *w_outs, gain.reshape(1, D_MODEL), wg.astype(BF16), wu.astype(BF16), wd.astype(BF16), fg)


def _odd_proj_kernel(x_ref, g_ref, w_ref, cos_ref, sin_ref, q_ref, k_ref, v_ref, xn_ref):
    xn = _rms(x_ref[...], g_ref[...]).astype(BF16)
    xn_ref[...] = xn
    cos = cos_ref[...]
    sin = sin_ref[...]
    half = RET_DK // 2

    def rot(z, out_ref, scale):
        for h in range(RET_HEADS):
            x1 = z[:, h * RET_DK:h * RET_DK + half]
            x2 = z[:, h * RET_DK + half:(h + 1) * RET_DK]
            out_ref[:, h * RET_DK:h * RET_DK + half] = ((x1 * cos - x2 * sin) * scale).astype(BF16)
            out_ref[:, h * RET_DK + half:(h + 1) * RET_DK] = ((x2 * cos + x1 * sin) * scale).astype(BF16)

    nq = RET_HEADS * RET_DK
    nv = RET_HEADS * RET_DV
    rot(_dot(xn, w_ref[:, 0:nq]), q_ref, 1.0)
    rot(_dot(xn, w_ref[:, nq:2 * nq]), k_ref, RET_DK ** -0.5)
    v_ref[...] = _dot(xn, w_ref[:, 2 * nq:2 * nq + nv]).astype(BF16)


def _odd_proj(h, gain, w_qkv, t):
    m = h.shape[0]
    tm = PROJ_TM
    nt = t // tm
    half = RET_DK // 2
    inv = ROPE_BASE ** (-jnp.arange(half, dtype=F32) / half)
    ang = jnp.arange(t, dtype=jnp.int32).astype(F32)[:, None] * inv[None, :]
    row = lambda n: pl.BlockSpec((tm, n), lambda i: (i, 0))
    pos = pl.BlockSpec((tm, half), lambda i: (i % nt, 0))
    nq = RET_HEADS * RET_DK
    nv = RET_HEADS * RET_DV
    return pl.pallas_call(
        _odd_proj_kernel,
        grid=(m // tm,),
        in_specs=[row(D_MODEL), _full((1, D_MODEL)), _resident(w_qkv.shape), pos, pos],
        out_specs=[row(nq), row(nq), row(nv), row(D_MODEL)],
        out_shape=[jax.ShapeDtypeStruct((m, nq), BF16), jax.ShapeDtypeStruct((m, nq), BF16),
                   jax.ShapeDtypeStruct((m, nv), BF16), jax.ShapeDtypeStruct((m, D_MODEL), BF16)],
        compiler_params=_cparams("parallel"),
        name="odd_proj",
    )(h, gain.reshape(1, D_MODEL), w_qkv, jnp.cos(ang), jnp.sin(ang))


def _ret_kernel(dec_ref, q_ref, k_ref, v_ref, xn_ref, wg_ref, dm_ref, xi_ref, ze_ref, rn_ref,
                o_ref, st_ref, *, n_chunks):
    c_ = RET_CHUNK

    @pl.when(pl.program_id(1) == 0)
    def _():
        st_ref[...] = jnp.zeros_like(st_ref)

    rn = rn_ref[...]
    for c in range(n_chunks):
        rows = slice(c * c_, (c + 1) * c_)
        for h in range(RET_HEADS):
            ks = slice(h * RET_DK, (h + 1) * RET_DK)
            vs = slice(h * RET_DV, (h + 1) * RET_DV)
            q = q_ref[rows, ks]
            k = k_ref[rows, ks]
            v = v_ref[rows, vs]
            st = st_ref[h]
            inner = (_dot_nt(q, k) * dm_ref[h]).astype(BF16)
            o = _dot(inner, v) + xi_ref[h] * _dot(q, st.astype(BF16))
            kz = (k.astype(F32) * ze_ref[h]).astype(BF16)
            st_ref[h] = dec_ref[h] * st + _dot_tn(kz, v)
            gate = _dot(xn_ref[rows, :], wg_ref[:, vs])
            y = _rms(o, rn) * _silu(gate)
            o_ref[rows, vs] = y.astype(o_ref.dtype)


def _retention(q, k, v, xn, w_gate, rnorm, b, t):
    c_ = min(RET_CHUNK, t)
    tt = min(RET_TT, t)
    nt = t // tt
    log_g = jnp.log1p(-jnp.exp2(-5.0 - jnp.arange(RET_HEADS, dtype=F32)))
    idx = jnp.arange(c_, dtype=F32)
    rel = idx[:, None] - idx[None, :]
    dmat = jnp.where(rel >= 0, jnp.exp(log_g[:, None, None] * jnp.maximum(rel, 0.0)), 0.0)
    xi = jnp.exp(log_g[:, None] * (idx[None, :] + 1.0))[:, :, None]
    zeta = jnp.exp(log_g[:, None] * (c_ - 1.0 - idx[None, :]))[:, :, None]
    decay_c = jnp.exp(log_g * c_)
    nq = RET_HEADS * RET_DK
    nv = RET_HEADS * RET_DV
    row = lambda n: pl.BlockSpec((tt, n), lambda i, j: (i * nt + j, 0))
    return pl.pallas_call(
        functools.partial(_ret_kernel, n_chunks=tt // c_),
        grid=(b, nt),
        in_specs=[pl.BlockSpec(memory_space=pltpu.SMEM),
                  row(nq), row(nq), row(nv), row(D_MODEL), _resident(w_gate.shape),
                  _full((RET_HEADS, c_, c_)), _full((RET_HEADS, c_, 1)), _full((RET_HEADS, c_, 1)),
                  _full((1, RET_DV))],
        out_specs=row(nv),
        out_shape=jax.ShapeDtypeStruct((b * t, nv), BF16),
        scratch_shapes=[pltpu.VMEM((RET_HEADS, RET_DK, RET_DV), F32)],
        compiler_params=_cparams("parallel", "arbitrary"),
        name="retention",
    )(decay_c, q, k, v, xn, w_gate, dmat, xi, zeta, rnorm.reshape(1, RET_DV))


def kernel(x, even_attn_norm, even_w_in, even_gla_wa2, even_gla_ba2, even_gla_norm, even_w_out,
           odd_attn_norm, odd_w_in, odd_ret_norm, odd_w_out,
           ffn_norm, ffn_w_gate, ffn_w_up, ffn_w_down, final_norm):
    b, t, d = x.shape
    h = x.reshape(b * t, d)

    gq, gk, gv, la, gr, dk, ik, dqt, iqt, dvt, iwt = _even_proj(
        h, even_attn_norm[0], even_w_in[0], even_gla_wa2[0], even_gla_ba2[0])
    o_gla = _gla(gq, gk, gv, la, gr, even_gla_norm[0], b, t)
    o_dsa = _dsa(iqt, dqt, iwt, ik, dk, dvt, b, t)
    w_out = even_w_out[0].astype(BF16)
    n_gla = GLA_HEADS * GLA_DV
    h = _mix_ffn(h, [o_gla, o_dsa], [w_out[:n_gla], w_out[n_gla:]],
                 ffn_norm[0], ffn_w_gate[0], ffn_w_up[0], ffn_w_down[0])

    w_odd = odd_w_in[0].astype(BF16)
    n_qkv = 2 * RET_HEADS * RET_DK + RET_HEADS * RET_DV
    q, k, v, xn = _odd_proj(h, odd_attn_norm[0], w_odd[:, :n_qkv], t)
    o_ret = _retention(q, k, v, xn, w_odd[:, n_qkv:], odd_ret_norm[0], b, t)
    h = _mix_ffn(h, [o_ret], [odd_w_out[0].astype(BF16)],
                 ffn_norm[1], ffn_w_gate[1], ffn_w_up[1], ffn_w_down[1], final_gain=final_norm)
    return h.reshape(b, t, d)
```

```python
import functools

import jax
import jax.numpy as jnp
from jax import lax
from jax.experimental import pallas as pl
from jax.experimental.pallas import tpu as pltpu

F32 = jnp.float32
BF16 = jnp.bfloat16

D_MODEL = 1024
EPS = 1e-6
GLA_HEADS = 4
GLA_DV = 128
GLA_DK = 64
GLA_RANK = 16
GLA_GATE_NORM = 16.0
GLA_CHUNK = 64
DSA_HEADS = 4
DSA_DH = 128
IDX_HEADS = 8
IDX_DIM = 64
TOPK_MAX = 256
RET_HEADS = 4
RET_DK = 256
RET_DV = 512
ROPE_BASE = 10000.0
D_FF = 2816

LANES = 128
SUBLANES = 8
VMEM_LIMIT = 56 * 1024 * 1024

PROJ_TM = 512
FFN_TM = 512
FFN_CHUNKS = ((0, 1024), (1024, 2048), (2048, 2816))
GLA_TT = 1024
RET_CHUNK = 256
RET_TT = 1024
DSA_TQ = 256
DSA_TK = 256
DSA_SK = 128
DSA_VROWS = DSA_DH + 16
LOG2E = 1.4426950408889634
PACKED_ROWS = 16
BISECT16_ITERS = 8
BISECT_ITERS = 8


def _cparams(*sem):
    return pltpu.CompilerParams(dimension_semantics=sem, vmem_limit_bytes=VMEM_LIMIT)


def _dot(a, b):
    return jnp.dot(a, b, preferred_element_type=F32)


def _dot_nt(a, b):
    return lax.dot_general(a, b, (((1,), (1,)), ((), ())), preferred_element_type=F32)


def _dot_tn(a, b):
    return lax.dot_general(a, b, (((0,), (0,)), ((), ())), preferred_element_type=F32)


def _rms(x, g):
    return x * lax.rsqrt(jnp.mean(x * x, axis=-1, keepdims=True) + EPS) * g


def _silu(x):
    return x * jax.nn.sigmoid(x)


def _full(shape):
    return pl.BlockSpec(shape, lambda *_: (0,) * len(shape))


def _resident(shape):
    return pl.BlockSpec(shape, lambda *_: (0,) * len(shape), pipeline_mode=pl.Buffered(1))


_E_GQ, _E_GK, _E_GV, _E_GA, _E_GR, _E_DK, _E_IK, _E_END = (0, 256, 512, 1024, 1152, 1664, 1792, 1920)
_T_DQ, _T_IQ, _T_DV, _T_IW, _T_END = (0, 512, 1024, 1152, 1168)


def _even_proj_kernel(x_ref, g_ref, w_ref, wt_ref, wa2_ref, ba2_ref,
                      gq_ref, gk_ref, gv_ref, la_ref, gr_ref, dk_ref, ik_ref,
                      dqt_ref, iqt_ref, dvt_ref, iwt_ref):
    xn = _rms(x_ref[...], g_ref[...]).astype(BF16)

    def proj(a, b):
        return _dot(xn, w_ref[:, a:b])

    def proj_t(a, b):
        return _dot_nt(wt_ref[a:b, :], xn)

    ga = proj(_E_GA, _E_GR).astype(BF16)
    gq_ref[...] = proj(_E_GQ, _E_GK)
    gk_ref[...] = proj(_E_GK, _E_GV)
    gv_ref[...] = proj(_E_GV, _E_GA).astype(BF16)
    z = _dot(ga, wa2_ref[...]) + ba2_ref[...]
    la_ref[...] = jax.nn.log_sigmoid(z) * (1.0 / GLA_GATE_NORM)
    gr_ref[...] = proj(_E_GR, _E_DK)
    dk_ref[...] = proj(_E_DK, _E_IK).astype(BF16)
    ik_ref[...] = proj(_E_IK, _E_IK + IDX_DIM).astype(BF16)
    dqt_ref[...] = (proj_t(_T_DQ, _T_IQ) * (DSA_DH ** -0.5 * LOG2E)).astype(BF16)
    iqt_ref[...] = proj_t(_T_IQ, _T_DV).astype(BF16)
    dvt = proj_t(_T_DV, _T_IW).astype(BF16)
    for c in range(dvt_ref.shape[0]):
        dvt_ref[c, 0:DSA_DH, :] = dvt[:, c * DSA_TK:(c + 1) * DSA_TK]
        dvt_ref[c, DSA_DH:DSA_VROWS, :] = jnp.ones((DSA_VROWS - DSA_DH, DSA_TK), BF16)
    iwt_ref[...] = proj_t(_T_IW, _T_END) * ((IDX_HEADS ** -0.5) * (IDX_DIM ** -0.5))


def _even_proj(x2, gain, w_in, wa2, ba2):
    m = x2.shape[0]
    tm = PROJ_TM
    s = [0, 256, 512, 1024, 1040, 1552, 2064, 2192, 2320, 2832, 2896, 2904]
    gq, gk, gv, ga, gr, dq, dk, dv, iq, ik, iw = [w_in[:, s[i]:s[i + 1]] for i in range(11)]
    zpad = lambda n: jnp.zeros((D_MODEL, n), w_in.dtype)
    w = jnp.concatenate([gq, gk, gv, ga, zpad(LANES - GLA_RANK), gr, dk, ik, zpad(LANES - IDX_DIM)],
                        axis=1).astype(BF16)
    wt = jnp.concatenate([dq, iq, dv, iw, zpad(_T_END - _T_IW - IDX_HEADS)], axis=1).T.astype(BF16)
    wa2p = jnp.concatenate(
        [wa2, jnp.zeros((LANES - GLA_RANK, wa2.shape[1]), wa2.dtype)], axis=0).astype(BF16)
    row = lambda n: pl.BlockSpec((tm, n), lambda i: (i, 0))
    col = lambda n: pl.BlockSpec((n, tm), lambda i: (0, i))
    n_iw = _T_END - _T_IW
    return pl.pallas_call(
        _even_proj_kernel,
        grid=(m // tm,),
        in_specs=[row(D_MODEL), _full((1, D_MODEL)), _full((D_MODEL, _E_END)),
                  _full((_T_END, D_MODEL)), _full((LANES, 256)), _full((1, 256))],
        out_specs=[row(256), row(256), row(512), row(256), row(512), row(DSA_DH), row(IDX_DIM),
                   col(512), col(512),
                   pl.BlockSpec((tm // DSA_TK, DSA_VROWS, DSA_TK), lambda i: (i, 0, 0)),
                   col(n_iw)],
        out_shape=[jax.ShapeDtypeStruct((m, 256), F32), jax.ShapeDtypeStruct((m, 256), F32),
                   jax.ShapeDtypeStruct((m, 512), BF16), jax.ShapeDtypeStruct((m, 256), F32),
                   jax.ShapeDtypeStruct((m, 512), F32), jax.ShapeDtypeStruct((m, DSA_DH), BF16),
                   jax.ShapeDtypeStruct((m, IDX_DIM), BF16),
                   jax.ShapeDtypeStruct((512, m), BF16), jax.ShapeDtypeStruct((512, m), BF16),
                   jax.ShapeDtypeStruct((m // DSA_TK, DSA_VROWS, DSA_TK), BF16),
                   jax.ShapeDtypeStruct((n_iw, m), F32)],
        compiler_params=_cparams("parallel"),
        name="even_proj",
    )(x2, gain.reshape(1, D_MODEL), w, wt, wa2p, ba2.reshape(1, 256))


def _gla_kernel(q_ref, k_ref, v_ref, la_ref, gr_ref, gn_ref, o_ref, st_ref, *, n_chunks):
    c_ = GLA_CHUNK

    @pl.when(pl.program_id(1) == 0)
    def _():
        st_ref[...] = jnp.zeros_like(st_ref)

    ri = lax.broadcasted_iota(jnp.int32, (c_, c_), 0)
    ci = lax.broadcasted_iota(jnp.int32, (c_, c_), 1)
    tri = ri >= ci
    ltri = jnp.where(tri, 1.0, 0.0).astype(BF16)
    gn = gn_ref[...]

    chunks = range(n_chunks)
    heads = range(GLA_HEADS)
    rows = [slice(c * c_, (c + 1) * c_) for c in chunks]
    vs = [slice(h * GLA_DV, (h + 1) * GLA_DV) for h in heads]

    def cumsum(g):
        g1 = g.astype(BF16)
        r1 = g - g1.astype(F32)
        g2 = r1.astype(BF16)
        g3 = (r1 - g2.astype(F32)).astype(BF16)
        return _dot(ltri, g1) + _dot(ltri, g2) + _dot(ltri, g3)

    low_half = lax.broadcasted_iota(jnp.int32, (c_, LANES), 1) < GLA_DK
    pair = [slice((h // 2) * LANES, (h // 2 + 1) * LANES) for h in heads]

    def half(z, h):
        zp = z[:, pair[h]]
        return (jnp.where(low_half, zp, 0.0) if h % 2 == 0 else jnp.where(low_half, 0.0, zp)).astype(BF16)

    cum = [cumsum(la_ref[rows[c], :]) for c in chunks]
    last = [cum[c][c_ - 1:c_, :] for c in chunks]
    q_t = [(q_ref[rows[c], :] * jnp.exp(cum[c]) * (GLA_DK ** -0.5)).astype(BF16) for c in chunks]
    k_t = [k_ref[rows[c], :] * jnp.exp(-cum[c]) for c in chunks]
    k_e = [k_ref[rows[c], :] * jnp.exp(last[c] - cum[c]) for c in chunks]
    dec = [jnp.exp(last[c]) for c in chunks]
    a = [[jnp.where(tri, _dot_nt(q_t[c][:, pair[h]], half(k_t[c], h)), 0.0).astype(BF16)
          for h in heads] for c in chunks]
    o_intra = [[_dot(a[c][h], v_ref[rows[c], vs[h]]) for h in heads] for c in chunks]
    kv = [[_dot_tn(v_ref[rows[c], vs[h]], half(k_e[c], h)) for h in heads] for c in chunks]
    for h in heads:
        st = st_ref[h]
        for c in chunks:
            o = o_intra[c][h] + _dot_nt(q_t[c][:, pair[h]], st.astype(BF16))
            st = dec[c][:, pair[h]] * st + kv[c][h]
            y = _rms(o, gn) * _silu(gr_ref[rows[c], vs[h]])
            o_ref[rows[c], vs[h]] = y.astype(o_ref.dtype)
        st_ref[h] = st


def _gla(gq, gk, gv, la, gr, gnorm, b, t):
    tt = min(GLA_TT, t)
    nt = t // tt
    row = lambda n: pl.BlockSpec((tt, n), lambda i, j: (i * nt + j, 0))
    return pl.pallas_call(
        functools.partial(_gla_kernel, n_chunks=tt // GLA_CHUNK),
        grid=(b, nt),
        in_specs=[row(256), row(256), row(512), row(256), row(512), _full((1, GLA_DV))],
        out_specs=row(512),
        out_shape=jax.ShapeDtypeStruct((b * t, 512), BF16),
        scratch_shapes=[pltpu.VMEM((GLA_HEADS, GLA_DV, LANES), F32)],
        compiler_params=_cparams("parallel", "arbitrary"),
        name="gla",
    )(gq, gk, gv, la, gr, gnorm.reshape(1, GLA_DV))


def _dsa_kernel(iqt_ref, dqt_ref, iwt_ref, ik_ref, dk_ref, dvt_ref, o_ref,
                s_ref, sb_ref, acc_ref, bias_ref, sbuf_ref, *, topk, t_len):
    tq, tk, sk = DSA_TQ, DSA_TK, DSA_SK
    n_sub = tk // sk
    qi = pl.program_id(1)
    q0 = qi * tq
    nkt = (q0 + tq + tk - 1) // tk
    neg_inf = -jnp.inf

    qpos = q0 + lax.broadcasted_iota(jnp.int32, (sk, tq), 1)
    krow = lax.broadcasted_iota(jnp.int32, (sk, tq), 0)
    krow_t = lax.broadcasted_iota(jnp.int32, (tk, tq), 0)

    def colsum(c):
        r = c.shape[0] // (4 * SUBLANES)
        part = jnp.sum(c.reshape(r, 4, SUBLANES, tq), axis=0)
        return jnp.sum(part, axis=0)

    def score_tile(kt, carry, diagonal):
        rmax, rmin = carry
        for j in range(n_sub):
            k0 = pl.multiple_of(kt * tk, tk) + j * sk
            ik_t = ik_ref[pl.ds(k0, sk), :]
            sc = jnp.zeros((sk, tq), F32)
            for h in range(IDX_HEADS):
                lg = _dot(ik_t, iqt_ref[h * IDX_DIM:(h + 1) * IDX_DIM, :])
                sc = sc + iwt_ref[h:h + 1, :] * jnp.maximum(lg, 0.0)
            if diagonal:
                allowed = (k0 + krow) <= qpos
                hi_part, lo_part = jnp.where(allowed, sc, neg_inf), jnp.where(allowed, sc, jnp.inf)
            else:
                hi_part, lo_part = sc, sc
            s_ref[kt, j * sk:(j + 1) * sk, :] = hi_part
            sb_ref[kt, j * sk:(j + 1) * sk, :] = hi_part.astype(BF16)
            rmax = jnp.maximum(rmax, jnp.max(hi_part.reshape(sk // SUBLANES, SUBLANES, tq), axis=0))
            rmin = jnp.minimum(rmin, jnp.min(lo_part.reshape(sk // SUBLANES, SUBLANES, tq), axis=0))
        return rmax, rmin

    def score_pair(i, carry):
        carry = score_tile(2 * i, carry, False)
        return score_tile(jnp.minimum(2 * i + 1, nkt - 2), carry, False)

    carry = lax.fori_loop(0, nkt // 2, score_pair,
                          (jnp.full((SUBLANES, tq), neg_inf, F32), jnp.full((SUBLANES, tq), jnp.inf, F32)))
    rmax, rmin = score_tile(nkt - 1, carry, True)
    rmax = jnp.max(rmax, axis=0, keepdims=True)
    rmin = jnp.min(rmin, axis=0, keepdims=True)

    def over_tiles(body, init):
        carry = lax.fori_loop(0, nkt // 2, lambda i, c: body(2 * i + 1, body(2 * i, c)), init)
        return lax.cond(nkt % 2 == 1, lambda: body(nkt - 1, carry), lambda: carry)

    def count(ind_fn):
        def body(kt, acc):
            return acc + colsum(ind_fn(s_ref[kt], kt))
        acc = over_tiles(body, jnp.zeros((SUBLANES, tq), F32))
        return jnp.sum(acc, axis=0, keepdims=True)

    kf = jnp.float32(topk)

    def count16(mid_b):
        def body(kt, acc):
            ind = jnp.where(sb_ref[kt] >= mid_b, jnp.ones((), BF16), jnp.zeros((), BF16))
            parts = [ind[r * PACKED_ROWS:(r + 1) * PACKED_ROWS, :] for r in range(tk // PACKED_ROWS)]
            while len(parts) > 1:
                parts = [parts[i] + parts[i + 1] for i in range(0, len(parts), 2)]
            return acc + parts[0].astype(F32)
        acc = over_tiles(body, jnp.zeros((PACKED_ROWS, tq), F32))
        return jnp.sum(acc, axis=0, keepdims=True)

    def bisect16(_, carry):
        lo, hi = carry
        mid_b = (lo + (hi - lo) * 0.5).astype(BF16)
        mid = mid_b.astype(F32)
        ge = count16(mid_b) >= kf
        below = mid - jnp.maximum(jnp.abs(mid) * 2.0 ** -7, 1e-30)
        return (jnp.where(ge, jnp.maximum(lo, below), lo), jnp.where(ge, hi, jnp.minimum(hi, mid)))

    def bisect(_, carry):
        lo, hi = carry
        mid = lo + (hi - lo) * 0.5
        ge = count(lambda s, kt: jnp.where(s >= mid, 1.0, 0.0)) >= kf
        return jnp.where(ge, mid, lo), jnp.where(ge, hi, mid)

    lo, hi = lax.fori_loop(0, BISECT16_ITERS, bisect16, (rmin, rmax))
    lo, _ = lax.fori_loop(0, BISECT_ITERS, bisect, (lo, hi))

    def colmin(c):
        return jnp.min(c.reshape(tk // SUBLANES, SUBLANES, tq), axis=0)

    def snap(kt, carry):
        cnt, mn = carry
        s = s_ref[kt]
        ge = s >= lo
        return cnt + colsum(jnp.where(ge, 1.0, 0.0)), jnp.minimum(mn, colmin(jnp.where(ge, s, jnp.inf)))

    c_lo, v0 = over_tiles(snap, (jnp.zeros((SUBLANES, tq), F32), jnp.full((SUBLANES, tq), jnp.inf, F32)))
    c_lo = jnp.sum(c_lo, axis=0, keepdims=True)
    v0 = jnp.min(v0, axis=0, keepdims=True)

    def count_above(v):
        return count(lambda s, kt: jnp.where(s > v, 1.0, 0.0))

    def next_above(v):
        def body(kt, mn):
            s = s_ref[kt]
            return jnp.minimum(mn, colmin(jnp.where(s > v, s, jnp.inf)))
        return jnp.min(over_tiles(body, jnp.full((SUBLANES, tq), jnp.inf, F32)), axis=0, keepdims=True)

    def pending(n_gt):
        return jnp.max(jnp.where(n_gt < kf, 0.0, 1.0))

    def step_up(carry):
        v, f_v, n_gt, _ = carry
        fin = n_gt < kf
        v = jnp.where(fin, v, next_above(v))
        f_v = jnp.where(fin, f_v, n_gt)
        n_gt = count_above(v)
        return v, f_v, n_gt, pending(n_gt)

    n_gt0 = count_above(v0)
    thr, f_thr, n_gt, _ = lax.while_loop(
        lambda carry: carry[3] > 0.0, step_up, (v0, c_lo, n_gt0, pending(n_gt0)))
    n_eq = f_thr - n_gt
    need = kf - n_gt

    @pl.when(jnp.max(n_eq - need) > 0.0)
    def _():
        def ibisect(_, carry):
            ilo, ihi = carry
            imid = jnp.right_shift(ilo + ihi, 1)
            c = count(lambda s, kt: jnp.where(
                s == thr, jnp.where((kt * tk + krow_t) <= imid, 1.0, 0.0), 0.0))
            ge = c >= need
            return jnp.where(ge, ilo, imid), jnp.where(ge, imid, ihi)
        ilo0 = jnp.full((1, tq), -1, jnp.int32)
        ihi0 = jnp.full((1, tq), t_len - 1, jnp.int32)
        n_it = max(1, int(t_len - 1).bit_length() + 1)
        jlim = lax.fori_loop(0, n_it, ibisect, (ilo0, ihi0))[1]

        def drop(kt, carry):
            s = s_ref[kt]
            s_ref[kt] = jnp.where(s == thr, jnp.where((kt * tk + krow_t) <= jlim, s, neg_inf), s)
            return carry
        lax.fori_loop(0, nkt, drop, 0)

    acc_ref[...] = jnp.zeros(acc_ref.shape, F32)

    def logits(t, slot):
        tc = jnp.minimum(t, nkt - 1)
        thr_eff = jnp.where(t < nkt, thr, jnp.inf)
        bias_ref[slot] = jnp.where(s_ref[tc] >= thr_eff, 0.0, neg_inf)
        k_t = dk_ref[pl.ds(pl.multiple_of(tc * tk, tk), tk), :]
        for h in range(DSA_HEADS):
            sbuf_ref[slot, h] = _dot(k_t, dqt_ref[h * DSA_DH:(h + 1) * DSA_DH, :]) + bias_ref[slot]

    def softmax_pv(t, slot, ms):
        ms = list(ms)
        vt_t = dvt_ref[jnp.minimum(t, nkt - 1)]
        for h in range(DSA_HEADS):
            m_new = jnp.maximum(ms[h], jnp.max(sbuf_ref[slot, h], axis=0, keepdims=True))
            m_safe = jnp.where(m_new == neg_inf, 0.0, m_new)
            alpha = jnp.exp2(ms[h] - m_safe)
            p = jnp.exp2(sbuf_ref[slot, h] - m_safe).astype(BF16)
            acc_ref[h] = alpha * acc_ref[h] + _dot(vt_t, p)
            ms[h] = m_new
        return tuple(ms)

    def attend(i, ms):
        logits(2 * i + 1, 1)
        ms = softmax_pv(2 * i, 0, ms)
        logits(2 * i + 2, 0)
        return softmax_pv(2 * i + 1, 1, ms)

    logits(0, 0)
    m0 = tuple(jnp.full((1, tq), neg_inf, F32) for _ in range(DSA_HEADS))
    lax.fori_loop(0, (nkt + 1) // 2, attend, m0)
    for h in range(DSA_HEADS):
        num = acc_ref[h, 0:DSA_DH, :]
        den = acc_ref[h, DSA_DH:DSA_DH + 1, :]
        o_ref[:, h * DSA_DH:(h + 1) * DSA_DH] = (num / den).T.astype(o_ref.dtype)


def _dsa(iqt, dqt, iwt, ik, dk, dvt, b, t):
    tq, tk, sk = DSA_TQ, DSA_TK, DSA_SK
    assert tq == tk and t % tq == 0
    nq = t // tq
    topk = min(TOPK_MAX, t // 4)
    qcol = lambda n: pl.BlockSpec((n, tq), lambda i, j: (0, i * nq + j))
    krow = lambda n: pl.BlockSpec((t, n), lambda i, j: (i, 0))
    return pl.pallas_call(
        functools.partial(_dsa_kernel, topk=topk, t_len=t),
        grid=(b, nq),
        in_specs=[qcol(512), qcol(512), qcol(iwt.shape[0]), krow(IDX_DIM), krow(DSA_DH),
                  pl.BlockSpec((t // tk, DSA_VROWS, tk), lambda i, j: (i, 0, 0))],
        out_specs=pl.BlockSpec((tq, 512), lambda i, j: (i * nq + j, 0)),
        out_shape=jax.ShapeDtypeStruct((b * t, 512), BF16),
        scratch_shapes=[
            pltpu.VMEM((t // tk, tk, tq), F32),
            pltpu.VMEM((t // tk, tk, tq), BF16),
            pltpu.VMEM((DSA_HEADS, DSA_VROWS, tq), F32),
            pltpu.VMEM((2, tk, tq), F32),
            pltpu.VMEM((2, DSA_HEADS, tk, tq), F32),
        ],
        compiler_params=_cparams("parallel", "arbitrary"),
        name="dsa",
    )(iqt, dqt, iwt, ik, dk, dvt)


def _mix_ffn_kernel(*refs, n_in, final_norm):
    res_ref = refs[0]
    a_refs = refs[1:1 + n_in]
    w_refs = refs[1 + n_in:1 + 2 * n_in]
    g_ref, wg_ref, wu_ref, wd_ref, fg_ref, o_ref, x_ref = refs[1 + 2 * n_in:]
    hm = res_ref.shape[0] // 2
    halves = [slice(r * hm, (r + 1) * hm) for r in range(2)]
    xns = []
    for rows in halves:
        x = res_ref[rows, :]
        for a, w in zip(a_refs, w_refs):
            x = x + _dot(a[rows, :], w[...])
        x_ref[rows, :] = x
        xns.append(_rms(x, g_ref[...]).astype(BF16))
    for rows, xn in zip(halves, xns):
        acc = None
        for a, b in FFN_CHUNKS:
            t = _silu(_dot(xn, wg_ref[:, a:b])) * _dot(xn, wu_ref[:, a:b])
            d = _dot(t.astype(BF16), wd_ref[a:b, :])
            acc = d if acc is None else acc + d
        y = x_ref[rows, :] + acc
        o_ref[rows, :] = _rms(y, fg_ref[...]) if final_norm else y


def _mix_ffn(res, acts, w_outs, gain, wg, wu, wd, final_gain=None):
    m = res.shape[0]
    tm = FFN_TM
    fg = jnp.ones((1, D_MODEL), F32) if final_gain is None else final_gain.reshape(1, D_MODEL)
    row = lambda n: pl.BlockSpec((tm, n), lambda i: (i, 0))
    return pl.pallas_call(
        functools.partial(_mix_ffn_kernel, n_in=len(acts), final_norm=final_gain is not None),
        grid=(m // tm,),
        in_specs=[row(D_MODEL)] + [row(a.shape[1]) for a in acts] + [_resident(w.shape) for w in w_outs]
                 + [_full((1, D_MODEL)), _resident((D_MODEL, D_FF)), _resident((D_MODEL, D_FF)),
                    _resident((D_FF, D_MODEL)), _full((1, D_MODEL))],
        out_specs=row(D_MODEL),
        out_shape=jax.ShapeDtypeStruct((m, D_MODEL), F32),
        scratch_shapes=[pltpu.VMEM((tm, D_MODEL), F32)],
        compiler_params=_cparams("parallel"),
        name="mix_ffn",
    )(res, *acts, *w_outs, gain.reshape(1, D_MODEL), wg.astype(BF16), wu.astype(BF16), wd.astype(BF16), fg)


def _odd_proj_kernel(x_ref, g_ref, w_ref, cos_ref, sin_ref, q_ref, k_ref, v_ref, gate_ref):
    xn = _rms(x_ref[...], g_ref[...]).astype(BF16)
    cos = cos_ref[...]
    sin = sin_ref[...]
    half = RET_DK // 2

    def rot(z, out_ref, scale):
        for h in range(RET_HEADS):
            x1 = z[:, h * RET_DK:h * RET_DK + half]
            x2 = z[:, h * RET_DK + half:(h + 1) * RET_DK]
            out_ref[:, h * RET_DK:h * RET_DK + half] = ((x1 * cos - x2 * sin) * scale).astype(BF16)
            out_ref[:, h * RET_DK + half:(h + 1) * RET_DK] = ((x2 * cos + x1 * sin) * scale).astype(BF16)

    nq = RET_HEADS * RET_DK
    nv = RET_HEADS * RET_DV
    rot(_dot(xn, w_ref[:, 0:nq]), q_ref, 1.0)
    rot(_dot(xn, w_ref[:, nq:2 * nq]), k_ref, RET_DK ** -0.5)
    v_ref[...] = _dot(xn, w_ref[:, 2 * nq:2 * nq + nv]).astype(BF16)
    gate_ref[...] = _dot(xn, w_ref[:, 2 * nq + nv:2 * nq + 2 * nv])


def _odd_proj(h, gain, w_in, t):
    m = h.shape[0]
    tm = PROJ_TM
    nt = t // tm
    half = RET_DK // 2
    inv = ROPE_BASE ** (-jnp.arange(half, dtype=F32) / half)
    ang = jnp.arange(t, dtype=jnp.int32).astype(F32)[:, None] * inv[None, :]
    row = lambda n: pl.BlockSpec((tm, n), lambda i: (i, 0))
    pos = pl.BlockSpec((tm, half), lambda i: (i % nt, 0))
    nq = RET_HEADS * RET_DK
    nv = RET_HEADS * RET_DV
    return pl.pallas_call(
        _odd_proj_kernel,
        grid=(m // tm,),
        in_specs=[row(D_MODEL), _full((1, D_MODEL)), _full(w_in.shape), pos, pos],
        out_specs=[row(nq), row(nq), row(nv), row(nv)],
        out_shape=[jax.ShapeDtypeStruct((m, nq), BF16), jax.ShapeDtypeStruct((m, nq), BF16),
                   jax.ShapeDtypeStruct((m, nv), BF16), jax.ShapeDtypeStruct((m, nv), F32)],
        compiler_params=_cparams("parallel"),
        name="odd_proj",
    )(h, gain.reshape(1, D_MODEL), w_in.astype(BF16), jnp.cos(ang), jnp.sin(ang))


def _ret_kernel(dec_ref, q_ref, k_ref, v_ref, gate_ref, dm_ref, xi_ref, ze_ref, rn_ref,
                o_ref, st_ref, *, n_chunks):
    c_ = RET_CHUNK

    @pl.when(pl.program_id(1) == 0)
    def _():
        st_ref[...] = jnp.zeros_like(st_ref)

    rn = rn_ref[...]
    for c in range(n_chunks):
        rows = slice(c * c_, (c + 1) * c_)
        for h in range(RET_HEADS):
            ks = slice(h * RET_DK, (h + 1) * RET_DK)
            vs = slice(h * RET_DV, (h + 1) * RET_DV)
            q = q_ref[rows, ks]
            k = k_ref[rows, ks]
            v = v_ref[rows, vs]
            st = st_ref[h]
            inner = (_dot_nt(q, k) * dm_ref[h]).astype(BF16)
            o = _dot(inner, v) + xi_ref[h] * _dot(q, st.astype(BF16))
            kz = (k.astype(F32) * ze_ref[h]).astype(BF16)
            st_ref[h] = dec_ref[h] * st + _dot_tn(kz, v)
            y = _rms(o, rn) * _silu(gate_ref[rows, vs])
            o_ref[rows, vs] = y.astype(o_ref.dtype)


def _retention(q, k, v, gate, rnorm, b, t):
    c_ = min(RET_CHUNK, t)
    tt = min(RET_TT, t)
    nt = t // tt
    log_g = jnp.log1p(-jnp.exp2(-5.0 - jnp.arange(RET_HEADS, dtype=F32)))
    idx = jnp.arange(c_, dtype=F32)
    rel = idx[:, None] - idx[None, :]
    dmat = jnp.where(rel >= 0, jnp.exp(log_g[:, None, None] * jnp.maximum(rel, 0.0)), 0.0)
    xi = jnp.exp(log_g[:, None] * (idx[None, :] + 1.0))[:, :, None]
    zeta = jnp.exp(log_g[:, None] * (c_ - 1.0 - idx[None, :]))[:, :, None]
    decay_c = jnp.exp(log_g * c_)
    nq = RET_HEADS * RET_DK
    nv = RET_HEADS * RET_DV
    row = lambda n: pl.BlockSpec((tt, n), lambda i, j: (i * nt + j, 0))
    return pl.pallas_call(
        functools.partial(_ret_kernel, n_chunks=tt // c_),
        grid=(b, nt),
        in_specs=[pl.BlockSpec(memory_space=pltpu.SMEM),
                  row(nq), row(nq), row(nv), row(nv),
                  _full((RET_HEADS, c_, c_)), _full((RET_HEADS, c_, 1)), _full((RET_HEADS, c_, 1)),
                  _full((1, RET_DV))],
        out_specs=row(nv),
        out_shape=jax.ShapeDtypeStruct((b * t, nv), BF16),
        scratch_shapes=[pltpu.VMEM((RET_HEADS, RET_DK, RET_DV), F32)],
        compiler_params=_cparams("parallel", "arbitrary"),
        name="retention",
    )(decay_c, q, k, v, gate, dmat, xi, zeta, rnorm.reshape(1, RET_DV))


def kernel(x, even_attn_norm, even_w_in, even_gla_wa2, even_gla_ba2, even_gla_norm, even_w_out,
           odd_attn_norm, odd_w_in, odd_ret_norm, odd_w_out,
           ffn_norm, ffn_w_gate, ffn_w_up, ffn_w_down, final_norm):
    b, t, d = x.shape
    h = x.reshape(b * t, d)

    gq, gk, gv, la, gr, dk, ik, dqt, iqt, dvt, iwt = _even_proj(
        h, even_attn_norm[0], even_w_in[0], even_gla_wa2[0], even_gla_ba2[0])
    o_gla = _gla(gq, gk, gv, la, gr, even_gla_norm[0], b, t)
    o_dsa = _dsa(iqt, dqt, iwt, ik, dk, dvt, b, t)
    w_out = even_w_out[0].astype(BF16)
    n_gla = GLA_HEADS * GLA_DV
    h = _mix_ffn(h, [o_gla, o_dsa], [w_out[:n_gla], w_out[n_gla:]],
                 ffn_norm[0], ffn_w_gate[0], ffn_w_up[0], ffn_w_down[0])

    q, k, v, gate = _odd_proj(h, odd_attn_norm[0], odd_w_in[0], t)
    o_ret = _retention(q, k, v, gate, odd_ret_norm[0], b, t)
    h = _mix_ffn(h, [o_ret], [odd_w_out[0].astype(BF16)],
                 ffn_norm[1], ffn_w_gate[1], ffn_w_up[1], ffn_w_down[1], final_gain=final_norm)
    return h.reshape(b, t, d)
```

```python
import functools

import jax
import jax.numpy as jnp
from jax import lax
from jax.experimental import pallas as pl
from jax.experimental.pallas import tpu as pltpu

F32 = jnp.float32
BF16 = jnp.bfloat16

D_MODEL = 1024
EPS = 1e-6
GLA_HEADS = 4
GLA_DV = 128
GLA_DK = 64
GLA_RANK = 16
GLA_GATE_NORM = 16.0
GLA_CHUNK = 64
DSA_HEADS = 4
DSA_DH = 128
IDX_HEADS = 8
IDX_DIM = 64
TOPK_MAX = 256
RET_HEADS = 4
RET_DK = 256
RET_DV = 512
ROPE_BASE = 10000.0
D_FF = 2816

LANES = 128
SUBLANES = 8
VMEM_LIMIT = 56 * 1024 * 1024

PROJ_TM = 512
FFN_TM = 512
FFN_CHUNKS = ((0, 1024), (1024, 2048), (2048, 2816))
GLA_TT = 1024
RET_CHUNK = 256
RET_TT = 1024
DSA_TQ = 256
DSA_TK = 256
DSA_SK = 128
DSA_VROWS = DSA_DH + 16
LOG2E = 1.4426950408889634
PACKED_ROWS = 16
BISECT16_ITERS = 8
BISECT_ITERS = 7


def _cparams(*sem):
    return pltpu.CompilerParams(dimension_semantics=sem, vmem_limit_bytes=VMEM_LIMIT)


def _dot(a, b):
    return jnp.dot(a, b, preferred_element_type=F32)


def _dot_nt(a, b):
    return lax.dot_general(a, b, (((1,), (1,)), ((), ())), preferred_element_type=F32)


def _dot_tn(a, b):
    return lax.dot_general(a, b, (((0,), (0,)), ((), ())), preferred_element_type=F32)


def _rms(x, g):
    return x * lax.rsqrt(jnp.mean(x * x, axis=-1, keepdims=True) + EPS) * g


def _silu(x):
    return x * jax.nn.sigmoid(x)


def _full(shape):
    return pl.BlockSpec(shape, lambda *_: (0,) * len(shape))


def _resident(shape):
    return pl.BlockSpec(shape, lambda *_: (0,) * len(shape), pipeline_mode=pl.Buffered(1))


_E_GQ, _E_GK, _E_GV, _E_GA, _E_GR, _E_DK, _E_IK, _E_END = (0, 256, 512, 1024, 1152, 1664, 1792, 1920)
_T_DQ, _T_IQ, _T_DV, _T_IW, _T_END = (0, 512, 1024, 1152, 1168)


def _even_proj_kernel(x_ref, g_ref, w_ref, wt_ref, wa2_ref, ba2_ref,
                      gq_ref, gk_ref, gv_ref, la_ref, gr_ref, dk_ref, ik_ref,
                      dqt_ref, iqt_ref, dvt_ref, iwt_ref):
    xn = _rms(x_ref[...], g_ref[...]).astype(BF16)

    def proj(a, b):
        return _dot(xn, w_ref[:, a:b])

    def proj_t(a, b):
        return _dot_nt(wt_ref[a:b, :], xn)

    ga = proj(_E_GA, _E_GR).astype(BF16)
    gq_ref[...] = proj(_E_GQ, _E_GK)
    gk_ref[...] = proj(_E_GK, _E_GV)
    gv_ref[...] = proj(_E_GV, _E_GA).astype(BF16)
    z = _dot(ga, wa2_ref[...]) + ba2_ref[...]
    la_ref[...] = jax.nn.log_sigmoid(z) * (1.0 / GLA_GATE_NORM)
    gr_ref[...] = proj(_E_GR, _E_DK)
    dk_ref[...] = proj(_E_DK, _E_IK).astype(BF16)
    ik_ref[...] = proj(_E_IK, _E_IK + IDX_DIM).astype(BF16)
    dqt_ref[...] = (proj_t(_T_DQ, _T_IQ) * (DSA_DH ** -0.5 * LOG2E)).astype(BF16)
    iqt_ref[...] = proj_t(_T_IQ, _T_DV).astype(BF16)
    dvt = proj_t(_T_DV, _T_IW).astype(BF16)
    for c in range(dvt_ref.shape[0]):
        dvt_ref[c, 0:DSA_DH, :] = dvt[:, c * DSA_TK:(c + 1) * DSA_TK]
        dvt_ref[c, DSA_DH:DSA_VROWS, :] = jnp.ones((DSA_VROWS - DSA_DH, DSA_TK), BF16)
    iwt_ref[...] = proj_t(_T_IW, _T_END) * ((IDX_HEADS ** -0.5) * (IDX_DIM ** -0.5))


def _even_proj(x2, gain, w_in, wa2, ba2):
    m = x2.shape[0]
    tm = PROJ_TM
    s = [0, 256, 512, 1024, 1040, 1552, 2064, 2192, 2320, 2832, 2896, 2904]
    gq, gk, gv, ga, gr, dq, dk, dv, iq, ik, iw = [w_in[:, s[i]:s[i + 1]] for i in range(11)]
    zpad = lambda n: jnp.zeros((D_MODEL, n), w_in.dtype)
    w = jnp.concatenate([gq, gk, gv, ga, zpad(LANES - GLA_RANK), gr, dk, ik, zpad(LANES - IDX_DIM)],
                        axis=1).astype(BF16)
    wt = jnp.concatenate([dq, iq, dv, iw, zpad(_T_END - _T_IW - IDX_HEADS)], axis=1).T.astype(BF16)
    wa2p = jnp.concatenate(
        [wa2, jnp.zeros((LANES - GLA_RANK, wa2.shape[1]), wa2.dtype)], axis=0).astype(BF16)
    row = lambda n: pl.BlockSpec((tm, n), lambda i: (i, 0))
    col = lambda n: pl.BlockSpec((n, tm), lambda i: (0, i))
    n_iw = _T_END - _T_IW
    return pl.pallas_call(
        _even_proj_kernel,
        grid=(m // tm,),
        in_specs=[row(D_MODEL), _full((1, D_MODEL)), _full((D_MODEL, _E_END)),
                  _full((_T_END, D_MODEL)), _full((LANES, 256)), _full((1, 256))],
        out_specs=[row(256), row(256), row(512), row(256), row(512), row(DSA_DH), row(IDX_DIM),
                   col(512), col(512),
                   pl.BlockSpec((tm // DSA_TK, DSA_VROWS, DSA_TK), lambda i: (i, 0, 0)),
                   col(n_iw)],
        out_shape=[jax.ShapeDtypeStruct((m, 256), F32), jax.ShapeDtypeStruct((m, 256), F32),
                   jax.ShapeDtypeStruct((m, 512), BF16), jax.ShapeDtypeStruct((m, 256), F32),
                   jax.ShapeDtypeStruct((m, 512), F32), jax.ShapeDtypeStruct((m, DSA_DH), BF16),
                   jax.ShapeDtypeStruct((m, IDX_DIM), BF16),
                   jax.ShapeDtypeStruct((512, m), BF16), jax.ShapeDtypeStruct((512, m), BF16),
                   jax.ShapeDtypeStruct((m // DSA_TK, DSA_VROWS, DSA_TK), BF16),
                   jax.ShapeDtypeStruct((n_iw, m), F32)],
        compiler_params=_cparams("parallel"),
        name="even_proj",
    )(x2, gain.reshape(1, D_MODEL), w, wt, wa2p, ba2.reshape(1, 256))


def _gla_kernel(q_ref, k_ref, v_ref, la_ref, gr_ref, gn_ref, o_ref, st_ref, *, n_chunks):
    c_ = GLA_CHUNK

    @pl.when(pl.program_id(1) == 0)
    def _():
        st_ref[...] = jnp.zeros_like(st_ref)

    ri = lax.broadcasted_iota(jnp.int32, (c_, c_), 0)
    ci = lax.broadcasted_iota(jnp.int32, (c_, c_), 1)
    tri = ri >= ci
    ltri = jnp.where(tri, 1.0, 0.0).astype(BF16)
    gn = gn_ref[...]

    chunks = range(n_chunks)
    heads = range(GLA_HEADS)
    rows = [slice(c * c_, (c + 1) * c_) for c in chunks]
    vs = [slice(h * GLA_DV, (h + 1) * GLA_DV) for h in heads]

    def cumsum(g):
        g1 = g.astype(BF16)
        r1 = g - g1.astype(F32)
        g2 = r1.astype(BF16)
        g3 = (r1 - g2.astype(F32)).astype(BF16)
        return _dot(ltri, g1) + _dot(ltri, g2) + _dot(ltri, g3)

    low_half = lax.broadcasted_iota(jnp.int32, (c_, LANES), 1) < GLA_DK
    pair = [slice((h // 2) * LANES, (h // 2 + 1) * LANES) for h in heads]

    def half(z, h):
        zp = z[:, pair[h]]
        return (jnp.where(low_half, zp, 0.0) if h % 2 == 0 else jnp.where(low_half, 0.0, zp)).astype(BF16)

    cum = [cumsum(la_ref[rows[c], :]) for c in chunks]
    last = [cum[c][c_ - 1:c_, :] for c in chunks]
    q_t = [(q_ref[rows[c], :] * jnp.exp(cum[c]) * (GLA_DK ** -0.5)).astype(BF16) for c in chunks]
    k_t = [k_ref[rows[c], :] * jnp.exp(-cum[c]) for c in chunks]
    k_e = [k_ref[rows[c], :] * jnp.exp(last[c] - cum[c]) for c in chunks]
    dec = [jnp.exp(last[c]) for c in chunks]
    a = [[jnp.where(tri, _dot_nt(q_t[c][:, pair[h]], half(k_t[c], h)), 0.0).astype(BF16)
          for h in heads] for c in chunks]
    o_intra = [[_dot(a[c][h], v_ref[rows[c], vs[h]]) for h in heads] for c in chunks]
    kv = [[_dot_tn(v_ref[rows[c], vs[h]], half(k_e[c], h)) for h in heads] for c in chunks]
    for h in heads:
        st = st_ref[h]
        for c in chunks:
            o = o_intra[c][h] + _dot_nt(q_t[c][:, pair[h]], st.astype(BF16))
            st = dec[c][:, pair[h]] * st + kv[c][h]
            y = _rms(o, gn) * _silu(gr_ref[rows[c], vs[h]])
            o_ref[rows[c], vs[h]] = y.astype(o_ref.dtype)
        st_ref[h] = st


def _gla(gq, gk, gv, la, gr, gnorm, b, t):
    tt = min(GLA_TT, t)
    nt = t // tt
    row = lambda n: pl.BlockSpec((tt, n), lambda i, j: (i * nt + j, 0))
    return pl.pallas_call(
        functools.partial(_gla_kernel, n_chunks=tt // GLA_CHUNK),
        grid=(b, nt),
        in_specs=[row(256), row(256), row(512), row(256), row(512), _full((1, GLA_DV))],
        out_specs=row(512),
        out_shape=jax.ShapeDtypeStruct((b * t, 512), BF16),
        scratch_shapes=[pltpu.VMEM((GLA_HEADS, GLA_DV, LANES), F32)],
        compiler_params=_cparams("parallel", "arbitrary"),
        name="gla",
    )(gq, gk, gv, la, gr, gnorm.reshape(1, GLA_DV))


def _dsa_kernel(iqt_ref, dqt_ref, iwt_ref, ik_ref, dk_ref, dvt_ref, o_ref,
                s_ref, sb_ref, acc_ref, bias_ref, sbuf_ref, *, topk, t_len):
    tq, tk, sk = DSA_TQ, DSA_TK, DSA_SK
    n_sub = tk // sk
    qi = pl.program_id(1)
    q0 = qi * tq
    nkt = (q0 + tq + tk - 1) // tk
    neg_inf = -jnp.inf

    qpos = q0 + lax.broadcasted_iota(jnp.int32, (sk, tq), 1)
    krow = lax.broadcasted_iota(jnp.int32, (sk, tq), 0)
    krow_t = lax.broadcasted_iota(jnp.int32, (tk, tq), 0)

    def colsum(c):
        r = c.shape[0] // (4 * SUBLANES)
        part = jnp.sum(c.reshape(r, 4, SUBLANES, tq), axis=0)
        return jnp.sum(part, axis=0)

    def score_tile(kt, carry, diagonal):
        rmax, rmin = carry
        for j in range(n_sub):
            k0 = pl.multiple_of(kt * tk, tk) + j * sk
            ik_t = ik_ref[pl.ds(k0, sk), :]
            sc = jnp.zeros((sk, tq), F32)
            for h in range(IDX_HEADS):
                lg = _dot(ik_t, iqt_ref[h * IDX_DIM:(h + 1) * IDX_DIM, :])
                sc = sc + iwt_ref[h:h + 1, :] * jnp.maximum(lg, 0.0)
            if diagonal:
                allowed = (k0 + krow) <= qpos
                hi_part, lo_part = jnp.where(allowed, sc, neg_inf), jnp.where(allowed, sc, jnp.inf)
            else:
                hi_part, lo_part = sc, sc
            s_ref[kt, j * sk:(j + 1) * sk, :] = hi_part
            sb_ref[kt, j * sk:(j + 1) * sk, :] = hi_part.astype(BF16)
            rmax = jnp.maximum(rmax, jnp.max(hi_part.reshape(sk // SUBLANES, SUBLANES, tq), axis=0))
            rmin = jnp.minimum(rmin, jnp.min(lo_part.reshape(sk // SUBLANES, SUBLANES, tq), axis=0))
        return rmax, rmin

    def score_pair(i, carry):
        carry = score_tile(2 * i, carry, False)
        return score_tile(jnp.minimum(2 * i + 1, nkt - 2), carry, False)

    carry = lax.fori_loop(0, nkt // 2, score_pair,
                          (jnp.full((SUBLANES, tq), neg_inf, F32), jnp.full((SUBLANES, tq), jnp.inf, F32)))
    rmax, rmin = score_tile(nkt - 1, carry, True)
    rmax = jnp.max(rmax, axis=0, keepdims=True)
    rmin = jnp.min(rmin, axis=0, keepdims=True)

    def over_tiles(body, init):
        carry = lax.fori_loop(0, nkt // 2, lambda i, c: body(2 * i + 1, body(2 * i, c)), init)
        return lax.cond(nkt % 2 == 1, lambda: body(nkt - 1, carry), lambda: carry)

    def count(ind_fn):
        def body(kt, acc):
            return acc + colsum(ind_fn(s_ref[kt], kt))
        acc = over_tiles(body, jnp.zeros((SUBLANES, tq), F32))
        return jnp.sum(acc, axis=0, keepdims=True)

    kf = jnp.float32(topk)

    def count16(mid_b):
        def body(kt, acc):
            ind = jnp.where(sb_ref[kt] >= mid_b, jnp.ones((), BF16), jnp.zeros((), BF16))
            parts = [ind[r * PACKED_ROWS:(r + 1) * PACKED_ROWS, :] for r in range(tk // PACKED_ROWS)]
            while len(parts) > 1:
                parts = [parts[i] + parts[i + 1] for i in range(0, len(parts), 2)]
            return acc + parts[0].astype(F32)
        acc = over_tiles(body, jnp.zeros((PACKED_ROWS, tq), F32))
        return jnp.sum(acc, axis=0, keepdims=True)

    def bisect16(_, carry):
        lo, hi = carry
        mid_b = (lo + (hi - lo) * 0.5).astype(BF16)
        mid = mid_b.astype(F32)
        ge = count16(mid_b) >= kf
        below = mid - jnp.maximum(jnp.abs(mid) * 2.0 ** -7, 1e-30)
        return (jnp.where(ge, jnp.maximum(lo, below), lo), jnp.where(ge, hi, jnp.minimum(hi, mid)))

    def bisect(_, carry):
        lo, hi = carry
        mid = lo + (hi - lo) * 0.5
        ge = count(lambda s, kt: jnp.where(s >= mid, 1.0, 0.0)) >= kf
        return jnp.where(ge, mid, lo), jnp.where(ge, hi, mid)

    lo, hi = lax.fori_loop(0, BISECT16_ITERS, bisect16, (rmin, rmax))
    lo, _ = lax.fori_loop(0, BISECT_ITERS, bisect, (lo, hi))

    def colmin(c):
        return jnp.min(c.reshape(tk // SUBLANES, SUBLANES, tq), axis=0)

    def snap(kt, carry):
        cnt, mn = carry
        s = s_ref[kt]
        ge = s >= lo
        return cnt + colsum(jnp.where(ge, 1.0, 0.0)), jnp.minimum(mn, colmin(jnp.where(ge, s, jnp.inf)))

    c_lo, v0 = over_tiles(snap, (jnp.zeros((SUBLANES, tq), F32), jnp.full((SUBLANES, tq), jnp.inf, F32)))
    c_lo = jnp.sum(c_lo, axis=0, keepdims=True)
    v0 = jnp.min(v0, axis=0, keepdims=True)

    def count_above(v):
        return count(lambda s, kt: jnp.where(s > v, 1.0, 0.0))

    def next_above(v):
        def body(kt, mn):
            s = s_ref[kt]
            return jnp.minimum(mn, colmin(jnp.where(s > v, s, jnp.inf)))
        return jnp.min(over_tiles(body, jnp.full((SUBLANES, tq), jnp.inf, F32)), axis=0, keepdims=True)

    def pending(n_gt):
        return jnp.max(jnp.where(n_gt < kf, 0.0, 1.0))

    def step_up(carry):
        v, f_v, n_gt, _ = carry
        fin = n_gt < kf
        v = jnp.where(fin, v, next_above(v))
        f_v = jnp.where(fin, f_v, n_gt)
        n_gt = count_above(v)
        return v, f_v, n_gt, pending(n_gt)

    n_gt0 = count_above(v0)
    thr, f_thr, n_gt, _ = lax.while_loop(
        lambda carry: carry[3] > 0.0, step_up, (v0, c_lo, n_gt0, pending(n_gt0)))
    n_eq = f_thr - n_gt
    need = kf - n_gt

    @pl.when(jnp.max(n_eq - need) > 0.0)
    def _():
        def ibisect(_, carry):
            ilo, ihi = carry
            imid = jnp.right_shift(ilo + ihi, 1)
            c = count(lambda s, kt: jnp.where(
                s == thr, jnp.where((kt * tk + krow_t) <= imid, 1.0, 0.0), 0.0))
            ge = c >= need
            return jnp.where(ge, ilo, imid), jnp.where(ge, imid, ihi)
        ilo0 = jnp.full((1, tq), -1, jnp.int32)
        ihi0 = jnp.full((1, tq), t_len - 1, jnp.int32)
        n_it = max(1, int(t_len - 1).bit_length() + 1)
        jlim = lax.fori_loop(0, n_it, ibisect, (ilo0, ihi0))[1]

        def drop(kt, carry):
            s = s_ref[kt]
            s_ref[kt] = jnp.where(s == thr, jnp.where((kt * tk + krow_t) <= jlim, s, neg_inf), s)
            return carry
        lax.fori_loop(0, nkt, drop, 0)

    acc_ref[...] = jnp.zeros(acc_ref.shape, F32)

    def logits(t, slot):
        tc = jnp.minimum(t, nkt - 1)
        thr_eff = jnp.where(t < nkt, thr, jnp.inf)
        bias_ref[slot] = jnp.where(s_ref[tc] >= thr_eff, 0.0, neg_inf)
        k_t = dk_ref[pl.ds(pl.multiple_of(tc * tk, tk), tk), :]
        for h in range(DSA_HEADS):
            sbuf_ref[slot, h] = _dot(k_t, dqt_ref[h * DSA_DH:(h + 1) * DSA_DH, :]) + bias_ref[slot]

    def softmax_pv(t, slot, ms):
        ms = list(ms)
        vt_t = dvt_ref[jnp.minimum(t, nkt - 1)]
        for h in range(DSA_HEADS):
            m_new = jnp.maximum(ms[h], jnp.max(sbuf_ref[slot, h], axis=0, keepdims=True))
            m_safe = jnp.where(m_new == neg_inf, 0.0, m_new)
            alpha = jnp.exp2(ms[h] - m_safe)
            p = jnp.exp2(sbuf_ref[slot, h] - m_safe).astype(BF16)
            acc_ref[h] = alpha * acc_ref[h] + _dot(vt_t, p)
            ms[h] = m_new
        return tuple(ms)

    def attend(i, ms):
        logits(2 * i + 1, 1)
        ms = softmax_pv(2 * i, 0, ms)
        logits(2 * i + 2, 0)
        return softmax_pv(2 * i + 1, 1, ms)

    logits(0, 0)
    m0 = tuple(jnp.full((1, tq), neg_inf, F32) for _ in range(DSA_HEADS))
    lax.fori_loop(0, (nkt + 1) // 2, attend, m0)
    for h in range(DSA_HEADS):
        num = acc_ref[h, 0:DSA_DH, :]
        den = acc_ref[h, DSA_DH:DSA_DH + 1, :]
        o_ref[:, h * DSA_DH:(h + 1) * DSA_DH] = (num / den).T.astype(o_ref.dtype)


def _dsa(iqt, dqt, iwt, ik, dk, dvt, b, t):
    tq, tk, sk = DSA_TQ, DSA_TK, DSA_SK
    assert tq == tk and t % tq == 0
    nq = t // tq
    topk = min(TOPK_MAX, t // 4)
    qcol = lambda n: pl.BlockSpec((n, tq), lambda i, j: (0, i * nq + j))
    krow = lambda n: pl.BlockSpec((t, n), lambda i, j: (i, 0))
    return pl.pallas_call(
        functools.partial(_dsa_kernel, topk=topk, t_len=t),
        grid=(b, nq),
        in_specs=[qcol(512), qcol(512), qcol(iwt.shape[0]), krow(IDX_DIM), krow(DSA_DH),
                  pl.BlockSpec((t // tk, DSA_VROWS, tk), lambda i, j: (i, 0, 0))],
        out_specs=pl.BlockSpec((tq, 512), lambda i, j: (i * nq + j, 0)),
        out_shape=jax.ShapeDtypeStruct((b * t, 512), BF16),
        scratch_shapes=[
            pltpu.VMEM((t // tk, tk, tq), F32),
            pltpu.VMEM((t // tk, tk, tq), BF16),
            pltpu.VMEM((DSA_HEADS, DSA_VROWS, tq), F32),
            pltpu.VMEM((2, tk, tq), F32),
            pltpu.VMEM((2, DSA_HEADS, tk, tq), F32),
        ],
        compiler_params=_cparams("parallel", "arbitrary"),
        name="dsa",
    )(iqt, dqt, iwt, ik, dk, dvt)


def _mix_ffn_kernel(*refs, n_in, final_norm):
    res_ref = refs[0]
    a_refs = refs[1:1 + n_in]
    w_refs = refs[1 + n_in:1 + 2 * n_in]
    g_ref, wg_ref, wu_ref, wd_ref, fg_ref, o_ref, x_ref = refs[1 + 2 * n_in:]
    hm = res_ref.shape[0] // 2
    halves = [slice(r * hm, (r + 1) * hm) for r in range(2)]
    xns = []
    for rows in halves:
        x = res_ref[rows, :]
        for a, w in zip(a_refs, w_refs):
            x = x + _dot(a[rows, :], w[...])
        x_ref[rows, :] = x
        xns.append(_rms(x, g_ref[...]).astype(BF16))
    for rows, xn in zip(halves, xns):
        acc = None
        for a, b in FFN_CHUNKS:
            t = _silu(_dot(xn, wg_ref[:, a:b])) * _dot(xn, wu_ref[:, a:b])
            d = _dot(t.astype(BF16), wd_ref[a:b, :])
            acc = d if acc is None else acc + d
        y = x_ref[rows, :] + acc
        o_ref[rows, :] = _rms(y, fg_ref[...]) if final_norm else y


def _mix_ffn(res, acts, w_outs, gain, wg, wu, wd, final_gain=None):
    m = res.shape[0]
    tm = FFN_TM
    fg = jnp.ones((1, D_MODEL), F32) if final_gain is None else final_gain.reshape(1, D_MODEL)
    row = lambda n: pl.BlockSpec((tm, n), lambda i: (i, 0))
    return pl.pallas_call(
        functools.partial(_mix_ffn_kernel, n_in=len(acts), final_norm=final_gain is not None),
        grid=(m // tm,),
        in_specs=[row(D_MODEL)] + [row(a.shape[1]) for a in acts] + [_resident(w.shape) for w in w_outs]
                 + [_full((1, D_MODEL)), _resident((D_MODEL, D_FF)), _resident((D_MODEL, D_FF)),
                    _resident((D_FF, D_MODEL)), _full((1, D_MODEL))],
        out_specs=row(D_MODEL),
        out_shape=jax.ShapeDtypeStruct((m, D_MODEL), F32),
        scratch_shapes=[pltpu.VMEM((tm, D_MODEL), F32)],
        compiler_params=_cparams("parallel"),
        name="mix_ffn",
    )(res, *acts, *w_outs, gain.reshape(1, D_MODEL), wg.astype(BF16), wu.astype(BF16), wd.astype(BF16), fg)


def _odd_proj_kernel(x_ref, g_ref, w_ref, cos_ref, sin_ref, q_ref, k_ref, v_ref, gate_ref):
    xn = _rms(x_ref[...], g_ref[...]).astype(BF16)
    cos = cos_ref[...]
    sin = sin_ref[...]
    half = RET_DK // 2

    def rot(z, out_ref, scale):
        for h in range(RET_HEADS):
            x1 = z[:, h * RET_DK:h * RET_DK + half]
            x2 = z[:, h * RET_DK + half:(h + 1) * RET_DK]
            out_ref[:, h * RET_DK:h * RET_DK + half] = ((x1 * cos - x2 * sin) * scale).astype(BF16)
            out_ref[:, h * RET_DK + half:(h + 1) * RET_DK] = ((x2 * cos + x1 * sin) * scale).astype(BF16)

    nq = RET_HEADS * RET_DK
    nv = RET_HEADS * RET_DV
    rot(_dot(xn, w_ref[:, 0:nq]), q_ref, 1.0)
    rot(_dot(xn, w_ref[:, nq:2 * nq]), k_ref, RET_DK ** -0.5)
    v_ref[...] = _dot(xn, w_ref[:, 2 * nq:2 * nq + nv]).astype(BF16)
    gate_ref[...] = _dot(xn, w_ref[:, 2 * nq + nv:2 * nq + 2 * nv])


def _odd_proj(h, gain, w_in, t):
    m = h.shape[0]
    tm = PROJ_TM
    nt = t // tm
    half = RET_DK // 2
    inv = ROPE_BASE ** (-jnp.arange(half, dtype=F32) / half)
    ang = jnp.arange(t, dtype=jnp.int32).astype(F32)[:, None] * inv[None, :]
    row = lambda n: pl.BlockSpec((tm, n), lambda i: (i, 0))
    pos = pl.BlockSpec((tm, half), lambda i: (i % nt, 0))
    nq = RET_HEADS * RET_DK
    nv = RET_HEADS * RET_DV
    return pl.pallas_call(
        _odd_proj_kernel,
        grid=(m // tm,),
        in_specs=[row(D_MODEL), _full((1, D_MODEL)), _full(w_in.shape), pos, pos],
        out_specs=[row(nq), row(nq), row(nv), row(nv)],
        out_shape=[jax.ShapeDtypeStruct((m, nq), BF16), jax.ShapeDtypeStruct((m, nq), BF16),
                   jax.ShapeDtypeStruct((m, nv), BF16), jax.ShapeDtypeStruct((m, nv), F32)],
        compiler_params=_cparams("parallel"),
        name="odd_proj",
    )(h, gain.reshape(1, D_MODEL), w_in.astype(BF16), jnp.cos(ang), jnp.sin(ang))


def _ret_kernel(dec_ref, q_ref, k_ref, v_ref, gate_ref, dm_ref, xi_ref, ze_ref, rn_ref,
                o_ref, st_ref, *, n_chunks):
    c_ = RET_CHUNK

    @pl.when(pl.program_id(1) == 0)
    def _():
        st_ref[...] = jnp.zeros_like(st_ref)

    rn = rn_ref[...]
    for c in range(n_chunks):
        rows = slice(c * c_, (c + 1) * c_)
        for h in range(RET_HEADS):
            ks = slice(h * RET_DK, (h + 1) * RET_DK)
            vs = slice(h * RET_DV, (h + 1) * RET_DV)
            q = q_ref[rows, ks]
            k = k_ref[rows, ks]
            v = v_ref[rows, vs]
            st = st_ref[h]
            inner = (_dot_nt(q, k) * dm_ref[h]).astype(BF16)
            o = _dot(inner, v) + xi_ref[h] * _dot(q, st.astype(BF16))
            kz = (k.astype(F32) * ze_ref[h]).astype(BF16)
            st_ref[h] = dec_ref[h] * st + _dot_tn(kz, v)
            y = _rms(o, rn) * _silu(gate_ref[rows, vs])
            o_ref[rows, vs] = y.astype(o_ref.dtype)


def _retention(q, k, v, gate, rnorm, b, t):
    c_ = min(RET_CHUNK, t)
    tt = min(RET_TT, t)
    nt = t // tt
    log_g = jnp.log1p(-jnp.exp2(-5.0 - jnp.arange(RET_HEADS, dtype=F32)))
    idx = jnp.arange(c_, dtype=F32)
    rel = idx[:, None] - idx[None, :]
    dmat = jnp.where(rel >= 0, jnp.exp(log_g[:, None, None] * jnp.maximum(rel, 0.0)), 0.0)
    xi = jnp.exp(log_g[:, None] * (idx[None, :] + 1.0))[:, :, None]
    zeta = jnp.exp(log_g[:, None] * (c_ - 1.0 - idx[None, :]))[:, :, None]
    decay_c = jnp.exp(log_g * c_)
    nq = RET_HEADS * RET_DK
    nv = RET_HEADS * RET_DV
    row = lambda n: pl.BlockSpec((tt, n), lambda i, j: (i * nt + j, 0))
    return pl.pallas_call(
        functools.partial(_ret_kernel, n_chunks=tt // c_),
        grid=(b, nt),
        in_specs=[pl.BlockSpec(memory_space=pltpu.SMEM),
                  row(nq), row(nq), row(nv), row(nv),
                  _full((RET_HEADS, c_, c_)), _full((RET_HEADS, c_, 1)), _full((RET_HEADS, c_, 1)),
                  _full((1, RET_DV))],
        out_specs=row(nv),
        out_shape=jax.ShapeDtypeStruct((b * t, nv), BF16),
        scratch_shapes=[pltpu.VMEM((RET_HEADS, RET_DK, RET_DV), F32)],
        compiler_params=_cparams("parallel", "arbitrary"),
        name="retention",
    )(decay_c, q, k, v, gate, dmat, xi, zeta, rnorm.reshape(1, RET_DV))


def kernel(x, even_attn_norm, even_w_in, even_gla_wa2, even_gla_ba2, even_gla_norm, even_w_out,
           odd_attn_norm, odd_w_in, odd_ret_norm, odd_w_out,
           ffn_norm, ffn_w_gate, ffn_w_up, ffn_w_down, final_norm):
    b, t, d = x.shape
    h = x.reshape(b * t, d)

    gq, gk, gv, la, gr, dk, ik, dqt, iqt, dvt, iwt = _even_proj(
        h, even_attn_norm[0], even_w_in[0], even_gla_wa2[0], even_gla_ba2[0])
    o_gla = _gla(gq, gk, gv, la, gr, even_gla_norm[0], b, t)
    o_dsa = _dsa(iqt, dqt, iwt, ik, dk, dvt, b, t)
    w_out = even_w_out[0].astype(BF16)
    n_gla = GLA_HEADS * GLA_DV
    h = _mix_ffn(h, [o_gla, o_dsa], [w_out[:n_gla], w_out[n_gla:]],
                 ffn_norm[0], ffn_w_gate[0], ffn_w_up[0], ffn_w_down[0])

    q, k, v, gate = _odd_proj(h, odd_attn_norm[0], odd_w_in[0], t)
    o_ret = _retention(q, k, v, gate, odd_ret_norm[0], b, t)
    h = _mix_ffn(h, [o_ret], [odd_w_out[0].astype(BF16)],
                 ffn_norm[1], ffn_w_gate[1], ffn_w_up[1], ffn_w_down[1], final_gain=final_norm)
    return h.reshape(b, t, d)
```
